```python
import math
import jax, jax.numpy as jnp
from jax import lax
import numpy as np

D_MODEL = 1024
BATCH = 16
SEQ = 4096
DEPTH = 1

HEAD_DIM = 64
NSA_KV_GROUPS = 2
NSA_HEADS = (D_MODEL // 2) // HEAD_DIM
NSA_HPG = NSA_HEADS // NSA_KV_GROUPS
NSA_WIDTH = NSA_HEADS * HEAD_DIM
NSA_BRANCHES = 3
CMP_BLOCK = 32
CMP_STRIDE = 16
CMP_HIDDEN = 256
SEL_BLOCK = 64
SEL_TOPN = 16
WINDOW = 512
DIFF_QK_DIM = 64
DIFF_V_DIM = 2 * DIFF_QK_DIM
DIFF_HEADS = (D_MODEL - NSA_WIDTH) // DIFF_V_DIM
DIFF_WIDTH = DIFF_HEADS * DIFF_V_DIM
MIX_WIDTH = NSA_WIDTH + DIFF_WIDTH
N_ATTN_HEADS = NSA_HEADS + DIFF_HEADS
OFF_NSA_KV = NSA_WIDTH
OFF_NSA_GATE = OFF_NSA_KV + NSA_BRANCHES * 2 * NSA_KV_GROUPS * HEAD_DIM
OFF_DIFF_Q = OFF_NSA_GATE + NSA_HEADS * NSA_BRANCHES
OFF_DIFF_K = OFF_DIFF_Q + DIFF_HEADS * 2 * DIFF_QK_DIM
OFF_DIFF_V = OFF_DIFF_K + DIFF_HEADS * 2 * DIFF_QK_DIM
IN_COLS = OFF_DIFF_V + DIFF_WIDTH
REL_BUCKETS = 32
REL_MAX_DIST = 128
PEER_HEADS = 8
PEER_NKEYS = 128
PEER_EXPERTS = PEER_NKEYS ** 2
PEER_TOPK = 16
PEER_QDIM = 256
QBLOCK = 128
PEER_CHUNK = 128
RMS_EPS = 1e-6
NEG_INF = -1e30
FORCED_SCORE = 1e9

kernel_name = 'hybrid_nsa_diffattn_peer_block'


def rmsnorm(x, g):
    xf = x.astype(jnp.float32)
    y = xf * lax.rsqrt(jnp.mean(xf * xf, axis=-1, keepdims=True) + RMS_EPS)
    return (y * g.astype(jnp.float32)).astype(x.dtype)


def t5_bucket(dist):
    n = jnp.maximum(dist, 0)
    max_exact = REL_BUCKETS // 2
    nf = jnp.maximum(n, 1).astype(jnp.float32)
    large = max_exact + (jnp.log(nf / max_exact) / math.log(REL_MAX_DIST / max_exact)
                         * (REL_BUCKETS - max_exact)).astype(jnp.int32)
    return jnp.where(n < max_exact, n, jnp.minimum(large, REL_BUCKETS - 1))


def masked_softmax(logits, mask):
    p = jax.nn.softmax(jnp.where(mask, logits, NEG_INF), axis=-1)
    return p * mask.astype(p.dtype)


def compress_blocks(tok, pos_enc, w1, w2, starts):
    idx = starts[:, None] + np.arange(CMP_BLOCK)[None, :]
    blocks = tok[:, idx] + pos_enc[None, None, :, None, :].astype(tok.dtype)
    b, nc = blocks.shape[:2]
    flat = blocks.transpose(0, 1, 3, 2, 4).reshape(b, nc, NSA_KV_GROUPS, CMP_BLOCK * HEAD_DIM)
    return jax.nn.gelu(flat @ w1) @ w2


def block_overlap(cmp_start, n_sel):
    sel_start = np.arange(n_sel) * SEL_BLOCK
    lo = np.maximum(cmp_start[:, None], sel_start[None, :])
    hi = np.minimum(cmp_start[:, None] + CMP_BLOCK, sel_start[None, :] + SEL_BLOCK)
    return np.maximum(hi - lo, 0).astype(np.float32) / CMP_BLOCK


def nsa_diff_mixer(h, w_in, w_out, rel_bias, cmp_pos, cmp_w1, cmp_w2,
                   lam_q1, lam_k1, lam_q2, lam_k2, subln_g, layer_idx):
    bsz, s, _ = h.shape
    dt = h.dtype
    f32 = jnp.float32
    G, HPG, DH = NSA_KV_GROUPS, NSA_HPG, HEAD_DIM
    proj = h @ w_in
    q_nsa = proj[..., :OFF_NSA_KV].reshape(bsz, s, G, HPG, DH)
    kv = proj[..., OFF_NSA_KV:OFF_NSA_GATE].reshape(bsz, s, NSA_BRANCHES, 2, G, DH)
    gates = jax.nn.sigmoid(proj[..., OFF_NSA_GATE:OFF_DIFF_Q]).reshape(bsz, s, G, HPG, NSA_BRANCHES)
    dq = proj[..., OFF_DIFF_Q:OFF_DIFF_K].reshape(bsz, s, DIFF_HEADS, 2, DIFF_QK_DIM)
    dk = proj[..., OFF_DIFF_K:OFF_DIFF_V].reshape(bsz, s, DIFF_HEADS, 2, DIFF_QK_DIM)
    dv = proj[..., OFF_DIFF_V:].reshape(bsz, s, DIFF_HEADS, DIFF_V_DIM)

    n_cmp = (s - CMP_BLOCK) // CMP_STRIDE + 1
    cmp_start = np.arange(n_cmp) * CMP_STRIDE
    k_cmp = compress_blocks(kv[:, :, 0, 0], cmp_pos[0], cmp_w1[0], cmp_w2[0], cmp_start)
    v_cmp = compress_blocks(kv[:, :, 0, 1], cmp_pos[1], cmp_w1[1], cmp_w2[1], cmp_start)
    cmp_end = jnp.asarray(cmp_start + CMP_BLOCK - 1, jnp.int32)
    n_sel = s // SEL_BLOCK
    n_top = min(SEL_TOPN, n_sel)
    cmp_to_sel = jnp.asarray(block_overlap(cmp_start, n_sel))
    k_sel = kv[:, :, 1, 0].reshape(bsz, n_sel, SEL_BLOCK, G, DH).transpose(0, 3, 1, 2, 4)
    v_sel = kv[:, :, 1, 1].reshape(bsz, n_sel, SEL_BLOCK, G, DH).transpose(0, 3, 1, 2, 4)
    pad = ((0, 0), (WINDOW, 0), (0, 0), (0, 0))
    k_win = jnp.pad(kv[:, :, 2, 0], pad)
    v_win = jnp.pad(kv[:, :, 2, 1], pad)

    tab_nsa = rel_bias[:, :NSA_HEADS].astype(f32).reshape(REL_BUCKETS, G, HPG)
    tab_nsa_g = tab_nsa.transpose(1, 0, 2)
    tab_diff = rel_bias[:, NSA_HEADS:].astype(f32)

    lam_init = 0.8 - 0.6 * math.exp(-0.3 * layer_idx)
    lam = (jnp.exp(jnp.sum(lam_q1.astype(f32) * lam_k1.astype(f32)))
           - jnp.exp(jnp.sum(lam_q2.astype(f32) * lam_k2.astype(f32))) + lam_init)

    scale = HEAD_DIM ** -0.5
    scale_d = DIFF_QK_DIM ** -0.5
    n_qb = s // QBLOCK
    q_offs = jnp.arange(QBLOCK)
    tok_offs = jnp.arange(SEL_BLOCK)
    blk_ids = jnp.arange(n_sel)
    win_offs = jnp.arange(WINDOW + QBLOCK)
    key_pos = jnp.arange(s)
    g_ids = jnp.arange(G)[:, None, None]

    def rows(a, b, start, size):
        return lax.dynamic_slice(a, (b, start) + (0,) * (a.ndim - 2), (1, size) + a.shape[2:])[0]

    def batch(a, b):
        return lax.dynamic_index_in_dim(a, b, 0, keepdims=False)

    def query_block(i):
        b = i // n_qb
        q0 = (i % n_qb) * QBLOCK
        t = q0 + q_offs
        q = rows(q_nsa, b, q0, QBLOCK)
        gt = rows(gates, b, q0, QBLOCK)

        d_c = t[:, None] - cmp_end[None, :]
        lg_c = (jnp.einsum('tghd,cgd->ghtc', q, batch(k_cmp, b)).astype(f32) * scale
                + tab_nsa[t5_bucket(d_c)].transpose(2, 3, 0, 1))
        p_c = masked_softmax(lg_c, d_c >= 0)
        o_c = jnp.einsum('ghtc,cgd->tghd', p_c.astype(dt), batch(v_cmp, b))

        imp = jnp.einsum('ghtc,cj->gtj', p_c, cmp_to_sel)
        cur = (t // SEL_BLOCK)[:, None]
        forced = (blk_ids == 0) | (blk_ids == cur) | (blk_ids == cur - 1)
        score = jnp.where(forced, FORCED_SCORE, jnp.where(blk_ids <= cur, imp, NEG_INF))
        _, sel = lax.top_k(score, n_top)
        sel_pos = sel[..., None] * SEL_BLOCK + tok_offs
        d_s = t[None, :, None, None] - sel_pos
        nk = n_top * SEL_BLOCK
        ks_g = batch(k_sel, b)[g_ids, sel].reshape(G, QBLOCK, nk, DH)
        vs_g = batch(v_sel, b)[g_ids, sel].reshape(G, QBLOCK, nk, DH)
        b_s = jnp.moveaxis(tab_nsa_g[g_ids[..., None], t5_bucket(d_s)], -1, 1).reshape(G, HPG, QBLOCK, nk)
        lg_s = jnp.einsum('tghd,gtkd->ghtk', q, ks_g).astype(f32) * scale + b_s
        p_s = masked_softmax(lg_s, (d_s >= 0).reshape(G, 1, QBLOCK, nk))
        o_s = jnp.einsum('ghtk,gtkd->tghd', p_s.astype(dt), vs_g)

        kp = q0 - WINDOW + win_offs
        d_w = t[:, None] - kp[None, :]
        ok_w = (d_w >= 0) & (d_w < WINDOW) & (kp >= 0)[None, :]
        lg_w = (jnp.einsum('tghd,sgd->ghts', q, rows(k_win, b, q0, WINDOW + QBLOCK)).astype(f32) * scale
                + tab_nsa[t5_bucket(d_w)].transpose(2, 3, 0, 1))
        p_w = masked_softmax(lg_w, ok_w)
        o_w = jnp.einsum('ghts,sgd->tghd', p_w.astype(dt), rows(v_win, b, q0, WINDOW + QBLOCK))

        o_nsa = (gt[..., 0:1] * o_c + gt[..., 1:2] * o_s + gt[..., 2:3] * o_w).reshape(QBLOCK, NSA_WIDTH)

        qd = rows(dq, b, q0, QBLOCK)
        d_d = t[:, None] - key_pos[None, :]
        lg_d = (jnp.einsum('thmd,shmd->mhts', qd, batch(dk, b)).astype(f32) * scale_d
                + tab_diff[t5_bucket(d_d)].transpose(2, 0, 1)[None])
        p_d = masked_softmax(lg_d, d_d >= 0)
        o_d = jnp.einsum('hts,shd->thd', (p_d[0] - lam * p_d[1]).astype(dt), batch(dv, b))
        o_d = rmsnorm(o_d, subln_g) * (1.0 - lam_init)
        return jnp.concatenate([o_nsa, o_d.reshape(QBLOCK, DIFF_WIDTH)], axis=-1)

    mixed = lax.map(query_block, jnp.arange(bsz * n_qb)).reshape(bsz, s, MIX_WIDTH)
    return mixed @ w_out


def peer_ffn(h, wq, sub_keys, u, v):
    bsz, s, d = h.shape
    dt = h.dtype
    chunks = h.reshape(bsz * s // PEER_CHUNK, PEER_CHUNK, d)
    kk = PEER_TOPK * PEER_TOPK

    def body(xc):
        q = (xc @ wq).reshape(PEER_CHUNK, PEER_HEADS, 2, PEER_QDIM // 2)
        sub = jnp.einsum('thpd,pkd->thpk', q, sub_keys).astype(jnp.float32)
        sv, si = lax.top_k(sub, PEER_TOPK)
        cand = (sv[:, :, 0, :, None] + sv[:, :, 1, None, :]).reshape(PEER_CHUNK, PEER_HEADS, kk)
        cidx = (si[:, :, 0, :, None] * PEER_NKEYS + si[:, :, 1, None, :]).reshape(PEER_CHUNK, PEER_HEADS, kk)
        best, pos = lax.top_k(cand, PEER_TOPK)
        eid = jnp.take_along_axis(cidx, pos, axis=-1)
        g = jax.nn.softmax(best, axis=-1).astype(dt)
        act = jax.nn.gelu(jnp.einsum('td,thkd->thk', xc, u[eid]))
        return jnp.einsum('thk,thkd->td', g * act, v[eid])

    return lax.map(body, chunks).reshape(bsz, s, d)


def setup_inputs(seed: int = 0) -> dict:
    key = jax.random.key(seed)
    ks = jax.random.split(key, 22)
    f32 = jnp.float32
    L, D = DEPTH, D_MODEL

    def nrm(k, shape, sd):
        return sd * jax.random.normal(k, shape, f32)

    return {
        'x': nrm(ks[0], (BATCH, SEQ, D), 1.0),
        'c': nrm(ks[1], (BATCH, D), 1.0),
        'rel_bias': nrm(ks[2], (REL_BUCKETS, N_ATTN_HEADS), 0.5),
        'ada_w': nrm(ks[3], (L, D, 6 * D), 0.5 * D ** -0.5),
        'ada_b': nrm(ks[4], (L, 6 * D), 0.02),
        'norm1_g': 1.0 + nrm(ks[5], (L, D), 0.02),
        'norm2_g': 1.0 + nrm(ks[6], (L, D), 0.02),
        'w_in': nrm(ks[7], (L, D, IN_COLS), D ** -0.5),
        'w_out': nrm(ks[8], (L, MIX_WIDTH, D), MIX_WIDTH ** -0.5),
        'cmp_pos': nrm(ks[9], (L, 2, CMP_BLOCK, HEAD_DIM), 0.02),
        'cmp_w1': nrm(ks[10], (L, 2, CMP_BLOCK * HEAD_DIM, CMP_HIDDEN), (CMP_BLOCK * HEAD_DIM) ** -0.5),
        'cmp_w2': nrm(ks[11], (L, 2, CMP_HIDDEN, HEAD_DIM), CMP_HIDDEN ** -0.5),
        'lam_q1': nrm(ks[12], (L, DIFF_QK_DIM), 0.1),
        'lam_k1': nrm(ks[13], (L, DIFF_QK_DIM), 0.1),
        'lam_q2': nrm(ks[14], (L, DIFF_QK_DIM), 0.1),
        'lam_k2': nrm(ks[15], (L, DIFF_QK_DIM), 0.1),
        'diff_subln_g': 1.0 + nrm(ks[16], (L, DIFF_V_DIM), 0.02),
        'peer_wq': nrm(ks[17], (L, D, PEER_HEADS * PEER_QDIM), D ** -0.5),
        'peer_sub_keys': nrm(ks[18], (L, 2, PEER_NKEYS, PEER_QDIM // 2), (PEER_QDIM // 2) ** -0.5),
        'peer_u': nrm(ks[19], (L, PEER_EXPERTS, D), D ** -0.5),
        'peer_v': nrm(ks[20], (L, PEER_EXPERTS, D), 1.0),
        'final_g': 1.0 + nrm(ks[21], (D,), 0.02),
    }


def reference(x, c, rel_bias, ada_w, ada_b, norm1_g, norm2_g, w_in, w_out, cmp_pos, cmp_w1, cmp_w2,
              lam_q1, lam_k1, lam_q2, lam_k2, diff_subln_g, peer_wq, peer_sub_keys, peer_u, peer_v, final_g):
    cond = jax.nn.silu(c)
    for l in range(DEPTH):
        mod = (cond @ ada_w[l] + ada_b[l]).reshape(c.shape[0], 6, 1, D_MODEL)
        sh_a, sc_a, g_a, sh_f, sc_f, g_f = (mod[:, i] for i in range(6))
        h = rmsnorm(x, norm1_g[l]) * (1 + sc_a) + sh_a
        x = x + g_a * nsa_diff_mixer(h, w_in[l], w_out[l], rel_bias, cmp_pos[l], cmp_w1[l], cmp_w2[l],
                                     lam_q1[l], lam_k1[l], lam_q2[l], lam_k2[l], diff_subln_g[l], l)
        h = rmsnorm(x, norm2_g[l]) * (1 + sc_f) + sh_f
        x = x + g_f * peer_ffn(h, peer_wq[l], peer_sub_keys[l], peer_u[l], peer_v[l])
    return rmsnorm(x, final_g)
```

```python
import math
from functools import partial

import numpy as np
import jax
import jax.numpy as jnp
from jax import lax
from jax.experimental import pallas as pl
from jax.experimental.pallas import tpu as pltpu

F32 = jnp.float32
BF16 = jnp.bfloat16

D_MODEL = 1024
HEAD_DIM = 64
NSA_GROUPS = 2
NSA_HPG = 4
NSA_HEADS = NSA_GROUPS * NSA_HPG
NSA_WIDTH = NSA_HEADS * HEAD_DIM
CMP_BLOCK = 32
CMP_STRIDE = 16
CMP_HIDDEN = 256
SEL_BLOCK = 64
SEL_TOPN = 16
WINDOW = 512
DIFF_HEADS = 4
DIFF_V_DIM = 128
DIFF_WIDTH = DIFF_HEADS * DIFF_V_DIM
N_ATTN_HEADS = NSA_HEADS + DIFF_HEADS
REL_BUCKETS = 32
REL_MAX_DIST = 128
PEER_HEADS = 8
PEER_NKEYS = 128
PEER_EXPERTS = PEER_NKEYS ** 2
PEER_TOPK = 16
RMS_EPS = 1e-6
NEG = -1e30
FORCED_SCORE = 1e9
LAM_INIT = 0.8 - 0.6 * math.exp(-0.3 * 0)

LANES = 128
TQ = 128
VMEM_LIMIT = 56 * 1024 * 1024

COL_QN = 0
COL_DQ = 512
COL_DK = 1024
COL_DV = 1536
COL_KVC = 2048
COL_SW = 2304
N_MAIN = 3328
N_GATE = 256
N_ALL = N_MAIN + N_GATE


def _dot(a, b, precision=None):
    return jnp.dot(a, b, preferred_element_type=F32, precision=precision)


def _dot_nt(a, b, precision=None):
    return lax.dot_general(a, b, (((1,), (1,)), ((), ())), preferred_element_type=F32, precision=precision)


def _params(sem):
    return pltpu.CompilerParams(dimension_semantics=sem, vmem_limit_bytes=VMEM_LIMIT)


def _mod_kernel(c_ref, w_ref, b_ref, o_ref):
    c = c_ref[...]
    cond = c * jax.nn.sigmoid(c)
    o_ref[...] = _dot(cond, w_ref[...], precision=lax.Precision.HIGHEST) + b_ref[...]


def _modulation(c, ada_w, ada_b):
    bsz, d = c.shape
    n = ada_w.shape[1]
    tn = 512
    return pl.pallas_call(
        _mod_kernel,
        out_shape=jax.ShapeDtypeStruct((bsz, n), F32),
        grid=(n // tn,),
        in_specs=[pl.BlockSpec((bsz, d), lambda j: (0, 0)),
                  pl.BlockSpec((d, tn), lambda j: (0, j)),
                  pl.BlockSpec((1, tn), lambda j: (0, j))],
        out_specs=pl.BlockSpec((bsz, tn), lambda j: (0, j)),
        compiler_params=_params(("arbitrary",)),
        name="modulation",
    )(c, ada_w, ada_b.reshape(1, n))


def _rms_mod(x, g, scale, shift):
    ms = jnp.mean(x * x, axis=-1, keepdims=True)
    y = x * lax.rsqrt(ms + RMS_EPS) * g
    return y * (1.0 + scale) + shift


def _inproj_kernel(x_ref, mod_ref, g_ref, w_ref, main_ref, gate_ref):
    mod = mod_ref[0]
    h = _rms_mod(x_ref[...], g_ref[...], mod[1:2], mod[0:1]).astype(BF16)
    step = 512
    for c0 in range(0, N_MAIN, step):
        c1 = min(c0 + step, N_MAIN)
        main_ref[:, c0:c1] = _dot(h, w_ref[:, c0:c1]).astype(BF16)
    gate_ref[...] = _dot(h, w_ref[:, N_MAIN:N_ALL])


def _in_proj(x2d, mod, g1, w_all, seq):
    t, d = x2d.shape
    tm = 512
    per_b = seq // tm
    return pl.pallas_call(
        _inproj_kernel,
        out_shape=(jax.ShapeDtypeStruct((t, N_MAIN), BF16), jax.ShapeDtypeStruct((t, N_GATE), F32)),
        grid=(t // tm,),
        in_specs=[pl.BlockSpec((tm, d), lambda i: (i, 0)),
                  pl.BlockSpec((1, 6, d), lambda i: (i // per_b, 0, 0)),
                  pl.BlockSpec((1, d), lambda i: (0, 0)),
                  pl.BlockSpec((d, N_ALL), lambda i: (0, 0))],
        out_specs=(pl.BlockSpec((tm, N_MAIN), lambda i: (i, 0)),
                   pl.BlockSpec((tm, N_GATE), lambda i: (i, 0))),
        compiler_params=_params(("arbitrary",)),
        name="in_proj",
    )(x2d, mod, g1, w_all)


def _compress_kernel(tok_ref, w1_ref, pos_ref, w2_ref, o_ref):
    tok = tok_ref[0, 0]
    half = CMP_STRIDE * HEAD_DIM
    first = _dot(tok, w1_ref[0, :half, :])
    second = _dot(tok, w1_ref[0, half:, :])
    posb = _dot(pos_ref[0], w1_ref[0].astype(F32), precision=lax.Precision.HIGHEST)
    hidden = first + jnp.concatenate([second[1:], second[:1]], axis=0) + posb
    o_ref[0, 0] = _dot(jax.nn.gelu(hidden).astype(BF16), w2_ref[0])


def _compress(tokr, w1, posflat, w2dup):
    bsz, _, nb, kdim = tokr.shape
    return pl.pallas_call(
        _compress_kernel,
        out_shape=jax.ShapeDtypeStruct((bsz, 4, nb, LANES), F32),
        grid=(bsz, 4),
        in_specs=[pl.BlockSpec((1, 1, nb, kdim), lambda b, j: (b, j, 0, 0)),
                  pl.BlockSpec((1, 2 * kdim, CMP_HIDDEN), lambda b, j: (j // 2, 0, 0)),
                  pl.BlockSpec((1, 1, 2 * kdim), lambda b, j: (j // 2, 0, 0)),
                  pl.BlockSpec((1, CMP_HIDDEN, LANES), lambda b, j: (j // 2, 0, 0))],
        out_specs=pl.BlockSpec((1, 1, nb, LANES), lambda b, j: (b, j, 0, 0)),
        compiler_params=_params(("arbitrary", "arbitrary")),
        name="compress",
    )(tokr, w1, posflat, w2dup)


def _flash_reset(m_ref, l_ref, acc_ref):
    m_ref[...] = jnp.full(m_ref.shape, 3.0 * NEG, F32)
    l_ref[...] = jnp.zeros(l_ref.shape, F32)
    acc_ref[...] = jnp.zeros(acc_ref.shape, F32)


def _flash_step(q_all, k, v, bias, m_ref, l_ref, acc_ref):
    s = _dot_nt(q_all, k) + bias
    m_old = m_ref[...]
    m_new = jnp.maximum(m_old, jnp.max(s, axis=1, keepdims=True))
    alpha = jnp.exp(m_old - m_new)
    p = jnp.exp(s - m_new)
    l_ref[...] = alpha * l_ref[...] + jnp.sum(p, axis=1, keepdims=True)
    acc_ref[...] = alpha * acc_ref[...] + _dot(p.astype(BF16), v)
    m_ref[...] = m_new


def _tile(ref, kt):
    return ref[pl.ds(pl.multiple_of(kt * TQ, TQ), TQ), :]


def _nsa_kernel(q_ref, kc_ref, vc_ref, ks_ref, vs_ref, kw_ref, vw_ref, gate_ref, bc_ref, bt_ref, cts_ref, ex_ref,
                o_ref, maskb_ref, m_ref, l_ref, acc_ref, *, n_top):
    qi = pl.program_id(2)
    rows = NSA_HPG * TQ
    lane = lax.broadcasted_iota(jnp.int32, (TQ, LANES), 1)
    low = lane < HEAD_DIM
    q = q_ref[...]
    zero = jnp.zeros((TQ, LANES), BF16)
    q_heads = []
    for h in range(NSA_HPG):
        blk = q[:, (h // 2) * LANES:(h // 2 + 1) * LANES]
        q_heads.append(jnp.where(low if h % 2 == 0 else ~low, blk, zero))
    q_all = jnp.concatenate(q_heads, axis=0)

    bias_c = bc_ref[...].reshape(rows, bc_ref.shape[-1])
    s_c = _dot_nt(q_all, kc_ref[0, 0].astype(BF16)) + bias_c
    valid = bias_c > 0.5 * NEG
    m_c = jnp.max(s_c, axis=1, keepdims=True)
    p_c = jnp.where(valid, jnp.exp(s_c - m_c), 0.0)
    l_c = jnp.sum(p_c, axis=1, keepdims=True)
    p_c = p_c / jnp.where(l_c > 0.0, l_c, 1.0)
    o_c = _dot(p_c.astype(BF16), vc_ref[0, 0].astype(BF16))

    p_sum = p_c[0:TQ] + p_c[TQ:2 * TQ] + p_c[2 * TQ:3 * TQ] + p_c[3 * TQ:4 * TQ]
    imp_t = _dot_nt(cts_ref[...], p_sum, precision=lax.Precision.HIGHEST)
    ns = imp_t.shape[0]
    blk_id = lax.broadcasted_iota(jnp.int32, (ns, TQ), 0)
    tpos = qi * TQ + lax.broadcasted_iota(jnp.int32, (ns, TQ), 1)
    cur = tpos // SEL_BLOCK
    forced = (blk_id == 0) | (blk_id == cur) | (blk_id == cur - 1)
    score = jnp.where(forced, FORCED_SCORE, jnp.where(blk_id <= cur, imp_t, NEG))
    rank = jnp.zeros((ns, TQ), F32)
    for k in range(ns):
        rk = score[k:k + 1, :]
        beats = (rk > score) | ((rk == score) & (blk_id > k))
        rank = rank + jnp.where(beats, 1.0, 0.0)
    sel_t = jnp.where(rank < float(n_top), 1.0, 0.0)
    sel = sel_t.T.astype(BF16)
    maskb_ref[...] = (_dot(sel, ex_ref[...]) - 1.0) * (-NEG)

    def bias_tile(idx):
        return bt_ref[:, idx].reshape(rows, TQ)

    def mask_tile(kt):
        mb = maskb_ref[:, pl.ds(pl.multiple_of(kt * TQ, TQ), TQ)]
        return jnp.concatenate([mb] * NSA_HPG, axis=0)

    _flash_reset(m_ref, l_ref, acc_ref)

    def far_body(kt, carry):
        _flash_step(q_all, _tile(ks_ref, kt), _tile(vs_ref, kt), bias_tile(3) + mask_tile(kt), m_ref, l_ref, acc_ref)
        return carry

    lax.fori_loop(0, jnp.maximum(qi - 1, 0), far_body, 0)

    @pl.when(qi >= 1)
    def _():
        _flash_step(q_all, _tile(ks_ref, qi - 1), _tile(vs_ref, qi - 1), bias_tile(1) + mask_tile(qi - 1),
                    m_ref, l_ref, acc_ref)

    _flash_step(q_all, _tile(ks_ref, qi), _tile(vs_ref, qi), bias_tile(0) + mask_tile(qi), m_ref, l_ref, acc_ref)
    o_s = acc_ref[...] / l_ref[...]

    _flash_reset(m_ref, l_ref, acc_ref)
    n_back = WINDOW // TQ
    for back in range(n_back, 0, -1):
        idx = 2 if back == n_back else (1 if back == 1 else 3)

        @pl.when(qi >= back)
        def _(back=back, idx=idx):
            _flash_step(q_all, _tile(kw_ref, qi - back), _tile(vw_ref, qi - back), bias_tile(idx), m_ref, l_ref, acc_ref)

    _flash_step(q_all, _tile(kw_ref, qi), _tile(vw_ref, qi), bias_tile(0), m_ref, l_ref, acc_ref)
    o_w = acc_ref[...] / l_ref[...]

    gt = jax.nn.sigmoid(gate_ref[...])
    outs = []
    for h in range(NSA_HPG):
        r = slice(h * TQ, (h + 1) * TQ)
        outs.append(gt[:, 3 * h:3 * h + 1] * o_c[r] + gt[:, 3 * h + 1:3 * h + 2] * o_s[r]
                    + gt[:, 3 * h + 2:3 * h + 3] * o_w[r])
    o_ref[:, 0:LANES] = jnp.where(low, outs[0], outs[1]).astype(BF16)
    o_ref[:, LANES:2 * LANES] = jnp.where(low, outs[2], outs[3]).astype(BF16)


def _nsa(main, kvcmp, gates, bias_c, bias_t, cts_t, expand, bsz, seq):
    nq = seq // TQ
    nb = kvcmp.shape[2]
    ns = cts_t.shape[0]
    swb = COL_SW // LANES
    kernel = partial(_nsa_kernel, n_top=min(SEL_TOPN, ns))
    return pl.pallas_call(
        kernel,
        out_shape=jax.ShapeDtypeStruct((bsz * seq, NSA_WIDTH), BF16),
        grid=(bsz, NSA_GROUPS, nq),
        in_specs=[
            pl.BlockSpec((TQ, 2 * LANES), lambda b, g, i: (b * nq + i, g)),
            pl.BlockSpec((1, 1, nb, LANES), lambda b, g, i: (b, g, 0, 0)),
            pl.BlockSpec((1, 1, nb, LANES), lambda b, g, i: (b, 2 + g, 0, 0)),
            pl.BlockSpec((seq, LANES), lambda b, g, i: (b, swb + g)),
            pl.BlockSpec((seq, LANES), lambda b, g, i: (b, swb + 2 + g)),
            pl.BlockSpec((seq, LANES), lambda b, g, i: (b, swb + 4 + g)),
            pl.BlockSpec((seq, LANES), lambda b, g, i: (b, swb + 6 + g)),
            pl.BlockSpec((TQ, LANES), lambda b, g, i: (b * nq + i, g)),
            pl.BlockSpec((NSA_HPG, TQ, nb), lambda b, g, i: (g, i, 0)),
            pl.BlockSpec((NSA_HPG, 4, TQ, TQ), lambda b, g, i: (g, 0, 0, 0)),
            pl.BlockSpec((ns, nb), lambda b, g, i: (0, 0)),
            pl.BlockSpec((ns, seq), lambda b, g, i: (0, 0)),
        ],
        out_specs=pl.BlockSpec((TQ, 2 * LANES), lambda b, g, i: (b * nq + i, g)),
        scratch_shapes=[pltpu.VMEM((TQ, seq), F32),
                        pltpu.VMEM((NSA_HPG * TQ, 1), F32),
                        pltpu.VMEM((NSA_HPG * TQ, 1), F32),
                        pltpu.VMEM((NSA_HPG * TQ, LANES), F32)],
        compiler_params=_params(("arbitrary", "arbitrary", "arbitrary")),
        name="nsa_attention",
    )(main, kvcmp, kvcmp, main, main, main, main, gates, bias_c, bias_t, cts_t, expand)


def _diff_kernel(q_ref, k_ref, v_ref, bt_ref, lam_ref, g_ref, o_ref, m_ref, l_ref, acc_ref):
    qi = pl.program_id(2)
    lane = lax.broadcasted_iota(jnp.int32, (TQ, LANES), 1)
    low = lane < HEAD_DIM
    q = q_ref[...]
    zero = jnp.zeros((TQ, LANES), BF16)
    q_all = jnp.concatenate([jnp.where(low, q, zero), jnp.where(low, zero, q)], axis=0)

    def bias_tile(idx):
        b = bt_ref[0, idx]
        return jnp.concatenate([b, b], axis=0)

    _flash_reset(m_ref, l_ref, acc_ref)

    def far_body(kt, carry):
        _flash_step(q_all, _tile(k_ref, kt), _tile(v_ref, kt), bias_tile(3), m_ref, l_ref, acc_ref)
        return carry

    lax.fori_loop(0, jnp.maximum(qi - 1, 0), far_body, 0)

    @pl.when(qi >= 1)
    def _():
        _flash_step(q_all, _tile(k_ref, qi - 1), _tile(v_ref, qi - 1), bias_tile(1), m_ref, l_ref, acc_ref)

    _flash_step(q_all, _tile(k_ref, qi), _tile(v_ref, qi), bias_tile(0), m_ref, l_ref, acc_ref)
    o = acc_ref[...] / l_ref[...]

    lam_rows = lam_ref[...]
    lam = (jnp.exp(jnp.sum(lam_rows[0:1] * lam_rows[1:2], axis=1, keepdims=True))
           - jnp.exp(jnp.sum(lam_rows[2:3] * lam_rows[3:4], axis=1, keepdims=True)) + LAM_INIT)
    od = o[0:TQ] - lam * o[TQ:2 * TQ]
    ms = jnp.mean(od * od, axis=-1, keepdims=True)
    o_ref[...] = (od * lax.rsqrt(ms + RMS_EPS) * g_ref[...] * (1.0 - LAM_INIT)).astype(BF16)


def _diff(main, bias_t, lam_rows, subln_g, bsz, seq):
    nq = seq // TQ
    return pl.pallas_call(
        _diff_kernel,
        out_shape=jax.ShapeDtypeStruct((bsz * seq, DIFF_WIDTH), BF16),
        grid=(bsz, DIFF_HEADS, nq),
        in_specs=[
            pl.BlockSpec((TQ, LANES), lambda b, h, i: (b * nq + i, COL_DQ // LANES + h)),
            pl.BlockSpec((seq, LANES), lambda b, h, i: (b, COL_DK // LANES + h)),
            pl.BlockSpec((seq, LANES), lambda b, h, i: (b, COL_DV // LANES + h)),
            pl.BlockSpec((1, 4, TQ, TQ), lambda b, h, i: (NSA_HEADS + h, 0, 0, 0)),
            pl.BlockSpec((8, LANES), lambda b, h, i: (0, 0)),
            pl.BlockSpec((1, LANES), lambda b, h, i: (0, 0)),
        ],
        out_specs=pl.BlockSpec((TQ, LANES), lambda b, h, i: (b * nq + i, h)),
        scratch_shapes=[pltpu.VMEM((2 * TQ, 1), F32),
                        pltpu.VMEM((2 * TQ, 1), F32),
                        pltpu.VMEM((2 * TQ, LANES), F32)],
        compiler_params=_params(("arbitrary", "arbitrary", "arbitrary")),
        name="diff_attention",
    )(main, main, main, bias_t, lam_rows, subln_g)


def _post_kernel(on_ref, od_ref, x_ref, mod_ref, g_ref, wo_ref, wq_ref, sk_ref, x1_ref, h2t_ref, subt_ref):
    mod = mod_ref[0]
    mixed = _dot(on_ref[...], wo_ref[0:NSA_WIDTH, :]) + _dot(od_ref[...], wo_ref[NSA_WIDTH:, :])
    x1 = x_ref[...] + mod[2:3] * mixed
    x1_ref[...] = x1
    h2 = _rms_mod(x1, g_ref[...], mod[4:5], mod[3:4])
    h2t_ref[...] = h2.T.astype(BF16)
    qp = _dot(h2.astype(BF16), wq_ref[...])
    for hp in range(2 * PEER_HEADS):
        qs = qp[:, hp * LANES:(hp + 1) * LANES]
        subt_ref[hp] = _dot_nt(sk_ref[hp % 2], qs, precision=lax.Precision.HIGHEST)


def _post(o_nsa, o_d, x2d, mod, g2, w_out, wq, sub_keys, seq):
    t, d = x2d.shape
    tm = 256
    per_b = seq // tm
    nq = wq.shape[1]
    return pl.pallas_call(
        _post_kernel,
        out_shape=(jax.ShapeDtypeStruct((t, d), F32),
                   jax.ShapeDtypeStruct((d, t), BF16),
                   jax.ShapeDtypeStruct((2 * PEER_HEADS, PEER_NKEYS, t), F32)),
        grid=(t // tm,),
        in_specs=[pl.BlockSpec((tm, NSA_WIDTH), lambda i: (i, 0)),
                  pl.BlockSpec((tm, DIFF_WIDTH), lambda i: (i, 0)),
                  pl.BlockSpec((tm, d), lambda i: (i, 0)),
                  pl.BlockSpec((1, 6, d), lambda i: (i // per_b, 0, 0)),
                  pl.BlockSpec((1, d), lambda i: (0, 0)),
                  pl.BlockSpec((d, d), lambda i: (0, 0)),
                  pl.BlockSpec((d, nq), lambda i: (0, 0)),
                  pl.BlockSpec((2, PEER_NKEYS, LANES), lambda i: (0, 0, 0))],
        out_specs=(pl.BlockSpec((tm, d), lambda i: (i, 0)),
                   pl.BlockSpec((d, tm), lambda i: (0, i)),
                   pl.BlockSpec((2 * PEER_HEADS, PEER_NKEYS, tm), lambda i: (0, 0, i))),
        compiler_params=_params(("arbitrary",)),
        name="post_attention",
    )(o_nsa, o_d, x2d, mod, g2, w_out, wq, sub_keys)


_PEER_PAIRS = [(r, c) for r in range(PEER_TOPK + 1) for c in range(PEER_TOPK + 1)
               if (r + 1) * (c + 1) <= PEER_TOPK + 1]


def _top_rows(x, n):
    out = []
    cur = x
    for _ in range(n):
        m = jnp.max(cur, axis=0, keepdims=True)
        out.append(m)
        cur = jnp.where(cur == m, -jnp.inf, cur)
    return out


def _peer_select_kernel(sub_ref, thr_ref, ea_ref, eb_ref, cand_ref):
    cand_ref[...] = jnp.full(cand_ref.shape, -jnp.inf, F32)
    for h in range(PEER_HEADS):
        a = sub_ref[2 * h]
        b = sub_ref[2 * h + 1]
        ta = _top_rows(a, PEER_TOPK + 1)
        tb = _top_rows(b, PEER_TOPK + 1)
        for n, (r, c) in enumerate(_PEER_PAIRS):
            cand_ref[n:n + 1, :] = ta[r] + tb[c]
        cand = cand_ref[...]
        top = _top_rows(cand, PEER_TOPK + 1)
        mid = 0.5 * (top[PEER_TOPK - 1] + top[PEER_TOPK])
        best = ta[0] + tb[0]
        z = jnp.sum(jnp.where(cand > mid, jnp.exp(cand - best), 0.0), axis=0, keepdims=True)
        thr_ref[h] = mid - a
        ea_ref[h] = jnp.exp(a - ta[0]) / z
        eb_ref[h] = jnp.exp(b - tb[0])


def _peer_select(subt):
    nhp, nk, t = subt.shape
    tt = 256
    shape = jax.ShapeDtypeStruct((PEER_HEADS, nk, t), F32)
    spec = pl.BlockSpec((PEER_HEADS, nk, tt), lambda i: (0, 0, i))
    return pl.pallas_call(
        _peer_select_kernel,
        out_shape=(shape, shape, shape),
        grid=(t // tt,),
        in_specs=[pl.BlockSpec((nhp, nk, tt), lambda i: (0, 0, i))],
        out_specs=(spec, spec, spec),
        scratch_shapes=[pltpu.VMEM((-(-len(_PEER_PAIRS) // 8) * 8, tt), F32)],
        compiler_params=_params(("arbitrary",)),
        name="peer_select",
    )(subt)


def _peer_dense_kernel(h2t_ref, u_ref, vt_ref, sub_ref, thr_ref, ea_ref, eb_ref, x1_ref, mod_ref, g_ref,
                       o_ref, acc_ref, y_ref, *, rows_per_step):
    j = pl.program_id(1)

    @pl.when(j == 0)
    def _():
        acc_ref[...] = jnp.zeros(acc_ref.shape, F32)

    for ii in range(rows_per_step):
        r = slice(ii * PEER_NKEYS, (ii + 1) * PEER_NKEYS)
        act = jax.nn.gelu(_dot(u_ref[r, :], h2t_ref[...]))
        w = None
        for h in range(PEER_HEADS):
            keep = sub_ref[h, 0] > thr_ref[h, ii:ii + 1, :]
            term = jnp.where(keep, eb_ref[h] * ea_ref[h, ii:ii + 1, :], 0.0)
            w = term if w is None else w + term
        y_ref[r, :] = (w * act).astype(BF16)
    acc_ref[...] += _dot(vt_ref[...], y_ref[...])

    @pl.when(j == pl.num_programs(1) - 1)
    def _():
        mod = mod_ref[0]
        x2 = x1_ref[...] + mod[5:6] * acc_ref[...].T
        ms = jnp.mean(x2 * x2, axis=-1, keepdims=True)
        o_ref[...] = x2 * lax.rsqrt(ms + RMS_EPS) * g_ref[...]


def _peer_dense(h2t, u_b, vt_b, subt, thr, ea, eb, x1, mod, final_g, seq):
    d, t = h2t.shape
    ne = u_b.shape[0]
    tt = 512
    ec = 1024
    rows = ec // PEER_NKEYS
    per_b = seq // tt
    nk = subt.shape[1]
    sub_b = subt.reshape(PEER_HEADS, 2, nk, t)
    kernel = partial(_peer_dense_kernel, rows_per_step=rows)
    return pl.pallas_call(
        kernel,
        out_shape=jax.ShapeDtypeStruct((t, d), F32),
        grid=(t // tt, ne // ec),
        in_specs=[pl.BlockSpec((d, tt), lambda i, j: (0, i)),
                  pl.BlockSpec((ec, d), lambda i, j: (j, 0)),
                  pl.BlockSpec((d, ec), lambda i, j: (0, j)),
                  pl.BlockSpec((PEER_HEADS, 1, nk, tt), lambda i, j: (0, 1, 0, i)),
                  pl.BlockSpec((PEER_HEADS, rows, tt), lambda i, j: (0, j, i)),
                  pl.BlockSpec((PEER_HEADS, rows, tt), lambda i, j: (0, j, i)),
                  pl.BlockSpec((PEER_HEADS, nk, tt), lambda i, j: (0, 0, i)),
                  pl.BlockSpec((tt, d), lambda i, j: (i, 0)),
                  pl.BlockSpec((1, 6, d), lambda i, j: (i // per_b, 0, 0)),
                  pl.BlockSpec((1, d), lambda i, j: (0, 0))],
        out_specs=pl.BlockSpec((tt, d), lambda i, j: (i, 0)),
        scratch_shapes=[pltpu.VMEM((d, tt), F32), pltpu.VMEM((ec, tt), BF16)],
        compiler_params=_params(("arbitrary", "arbitrary")),
        name="peer_dense",
    )(h2t, u_b, vt_b, sub_b, thr, ea, eb, x1, mod, final_g)


def _t5_bucket_np(dist):
    n = np.maximum(dist, 0)
    max_exact = REL_BUCKETS // 2
    nf = np.maximum(n, 1).astype(np.float32)
    large = max_exact + (np.log(nf / np.float32(max_exact)) / np.float32(math.log(REL_MAX_DIST / max_exact))
                         * np.float32(REL_BUCKETS - max_exact)).astype(np.int32)
    return np.where(n < max_exact, n, np.minimum(large, REL_BUCKETS - 1)).astype(np.int32)


def _token_bias_tables():
    i = np.arange(TQ)[:, None]
    j = np.arange(TQ)[None, :]
    d0 = i - j
    d1 = TQ + i - j
    d2 = WINDOW + i - j
    d3 = 2 * TQ + i - j
    dist = np.stack([d0, d1, d2, d3])
    valid = np.stack([d0 >= 0, np.ones_like(d0, bool), d2 < WINDOW, np.ones_like(d0, bool)])
    return _t5_bucket_np(dist), valid


def _cmp_bias_tables(seq, nb):
    t = np.arange(seq)[:, None]
    c = np.arange(nb)[None, :]
    d = t - (c * CMP_STRIDE + CMP_BLOCK - 1)
    valid = (d >= 0) & (c < nb - 1)
    return _t5_bucket_np(d), valid


def _block_overlap_t(nb, n_sel):
    cmp_start = np.arange(nb) * CMP_STRIDE
    sel_start = np.arange(n_sel) * SEL_BLOCK
    lo = np.maximum(cmp_start[None, :], sel_start[:, None])
    hi = np.minimum(cmp_start[None, :] + CMP_BLOCK, sel_start[:, None] + SEL_BLOCK)
    ov = np.maximum(hi - lo, 0).astype(np.float32) / CMP_BLOCK
    ov[:, nb - 1] = 0.0
    return ov


def _arrange_w_in(w):
    d = w.shape[0]
    scale = HEAD_DIM ** -0.5
    o_kv = NSA_WIDTH
    o_gate = o_kv + 3 * 2 * NSA_GROUPS * HEAD_DIM
    o_dq = o_gate + NSA_HEADS * 3
    o_dk = o_dq + DIFF_WIDTH
    o_dv = o_dk + DIFF_WIDTH
    qn = w[:, :o_kv] * scale
    kv = w[:, o_kv:o_gate].reshape(d, 3, 2, NSA_GROUPS, HEAD_DIM)
    gates = w[:, o_gate:o_dq].reshape(d, NSA_GROUPS, NSA_HPG * 3)
    dq = w[:, o_dq:o_dk] * scale
    dk = w[:, o_dk:o_dv]
    dv = w[:, o_dv:]
    kvc = kv[:, 0].reshape(d, 2 * NSA_GROUPS * HEAD_DIM)
    dup = jnp.concatenate([kv[:, 1:], kv[:, 1:]], axis=-1).reshape(d, 2 * 2 * NSA_GROUPS * LANES)
    gpad = jnp.pad(gates, ((0, 0), (0, 0), (0, LANES - NSA_HPG * 3))).reshape(d, N_GATE)
    return jnp.concatenate([qn, dq, dk, dv, kvc, dup, gpad], axis=1).astype(BF16)


def kernel(x, c, rel_bias, ada_w, ada_b, norm1_g, norm2_g, w_in, w_out, cmp_pos, cmp_w1, cmp_w2, lam_q1, lam_k1,
           lam_q2, lam_k2, diff_subln_g, peer_wq, peer_sub_keys, peer_u, peer_v, final_g):
    bsz, seq, d = x.shape
    t = bsz * seq
    nb = seq // CMP_STRIDE
    n_sel = seq // SEL_BLOCK
    x2d = x.reshape(t, d)

    mod = _modulation(c, ada_w[0], ada_b[0]).reshape(bsz, 6, d)
    main, gates = _in_proj(x2d, mod, norm1_g[0].reshape(1, d), _arrange_w_in(w_in[0]), seq)

    tokr = (main[:, COL_KVC:COL_KVC + 4 * HEAD_DIM].reshape(bsz, nb, CMP_STRIDE, 4, HEAD_DIM)
            .transpose(0, 3, 1, 2, 4).reshape(bsz, 4, nb, CMP_STRIDE * HEAD_DIM))
    w2dup = jnp.concatenate([cmp_w2[0], cmp_w2[0]], axis=-1).astype(BF16)
    kvcmp = _compress(tokr, cmp_w1[0].astype(BF16), cmp_pos[0].reshape(2, 1, CMP_BLOCK * HEAD_DIM), w2dup)

    tab = rel_bias.astype(F32).T
    bkt_t, valid_t = _token_bias_tables()
    bias_t = jnp.where(valid_t[None], tab[:, bkt_t], NEG)
    bkt_c, valid_c = _cmp_bias_tables(seq, nb)
    bias_c = jnp.where(valid_c[None], tab[:NSA_HEADS][:, bkt_c], NEG)
    cts_t = jnp.asarray(_block_overlap_t(nb, n_sel))
    expand = jnp.asarray(np.repeat(np.eye(n_sel, dtype=np.float32), SEL_BLOCK, axis=1), BF16)

    o_nsa = _nsa(main, kvcmp, gates, bias_c, bias_t, cts_t, expand, bsz, seq)

    lam_rows = jnp.pad(jnp.stack([lam_q1[0], lam_k1[0], lam_q2[0], lam_k2[0]]).astype(F32),
                       ((0, 4), (0, LANES - lam_q1.shape[1])))
    o_d = _diff(main, bias_t, lam_rows, diff_subln_g[0].reshape(1, DIFF_V_DIM), bsz, seq)

    x1, h2t, subt = _post(o_nsa, o_d, x2d, mod, norm2_g[0].reshape(1, d), w_out[0].astype(BF16),
                          peer_wq[0].astype(BF16), peer_sub_keys[0], seq)
    thr, ea, eb = _peer_select(subt)
    out = _peer_dense(h2t, peer_u[0].astype(BF16), peer_v[0].T.astype(BF16), subt, thr, ea, eb, x1, mod,
                      final_g.reshape(1, d), seq)
    return out.reshape(bsz, seq, d)
```

```python
import math
from functools import partial

import numpy as np
import jax
import jax.numpy as jnp
from jax import lax
from jax.experimental import pallas as pl
from jax.experimental.pallas import tpu as pltpu

F32 = jnp.float32
BF16 = jnp.bfloat16

D_MODEL = 1024
HEAD_DIM = 64
NSA_GROUPS = 2
NSA_HPG = 4
NSA_HEADS = NSA_GROUPS * NSA_HPG
NSA_WIDTH = NSA_HEADS * HEAD_DIM
CMP_BLOCK = 32
CMP_STRIDE = 16
CMP_HIDDEN = 256
SEL_BLOCK = 64
SEL_TOPN = 16
WINDOW = 512
DIFF_HEADS = 4
DIFF_V_DIM = 128
DIFF_WIDTH = DIFF_HEADS * DIFF_V_DIM
N_ATTN_HEADS = NSA_HEADS + DIFF_HEADS
REL_BUCKETS = 32
REL_MAX_DIST = 128
PEER_HEADS = 8
PEER_NKEYS = 128
PEER_EXPERTS = PEER_NKEYS ** 2
PEER_TOPK = 16
RMS_EPS = 1e-6
NEG = -1e30
FORCED_SCORE = 1e9
LAM_INIT = 0.8 - 0.6 * math.exp(-0.3 * 0)

LANES = 128
TQ = 256
TK_FAR = 512
CMP_NEAR = 32
VMEM_LIMIT = 56 * 1024 * 1024

COL_QN = 0
COL_DQ = 1024
COL_DK = 2048
COL_DV = 2560
COL_KVC = 3072
COL_KSEL = 3328
COL_KWIN = 3584
COL_VSEL = 3840
COL_VWIN = 4096
N_MAIN = 4352
N_GATE = 256
N_ALL = N_MAIN + N_GATE


def _dot(a, b, precision=None):
    return jnp.dot(a, b, preferred_element_type=F32, precision=precision)


def _dot_nt(a, b, precision=None):
    return lax.dot_general(a, b, (((1,), (1,)), ((), ())), preferred_element_type=F32, precision=precision)


def _params(sem):
    return pltpu.CompilerParams(dimension_semantics=sem, vmem_limit_bytes=VMEM_LIMIT)


def _mod_kernel(c_ref, w_ref, b_ref, o_ref):
    c = c_ref[...]
    cond = c * jax.nn.sigmoid(c)
    o_ref[...] = _dot(cond, w_ref[...], precision=lax.Precision.HIGHEST) + b_ref[...]


def _modulation(c, ada_w, ada_b):
    bsz, d = c.shape
    n = ada_w.shape[1]
    tn = 512
    return pl.pallas_call(
        _mod_kernel,
        out_shape=jax.ShapeDtypeStruct((bsz, n), F32),
        grid=(n // tn,),
        in_specs=[pl.BlockSpec((bsz, d), lambda j: (0, 0)),
                  pl.BlockSpec((d, tn), lambda j: (0, j)),
                  pl.BlockSpec((1, tn), lambda j: (0, j))],
        out_specs=pl.BlockSpec((bsz, tn), lambda j: (0, j)),
        compiler_params=_params(("arbitrary",)),
        name="modulation",
    )(c, ada_w, ada_b.reshape(1, n))


def _rms_mod(x, g, scale, shift):
    ms = jnp.mean(x * x, axis=-1, keepdims=True)
    y = x * lax.rsqrt(ms + RMS_EPS) * g
    return y * (1.0 + scale) + shift


def _inproj_kernel(x_ref, mod_ref, g_ref, w_ref, main_ref, gate_ref, *, tiles_per_seq):
    mod = mod_ref[0]
    h = _rms_mod(x_ref[...], g_ref[...], mod[1:2], mod[0:1]).astype(BF16)
    tm = h.shape[0]
    step = 512
    for c0 in range(0, N_MAIN, step):
        c1 = min(c0 + step, N_MAIN)
        main_ref[:, c0:c1] = _dot(h, w_ref[:, c0:c1]).astype(BF16)
    gate_ref[...] = _dot(h, w_ref[:, N_MAIN:N_ALL])
    pos = (pl.program_id(0) % tiles_per_seq) * tm + lax.broadcasted_iota(jnp.int32, (tm, LANES), 0)
    lane = lax.broadcasted_iota(jnp.int32, (tm, LANES), 1)
    onehot = jnp.where(lane == HEAD_DIM + pos // SEL_BLOCK, 1.0, 0.0).astype(BF16)
    for g in range(NSA_GROUPS):
        c0 = COL_KSEL + g * LANES
        main_ref[:, c0:c0 + LANES] = main_ref[:, c0:c0 + LANES] + onehot


def _in_proj(x2d, mod, g1, w_all, seq):
    t, d = x2d.shape
    tm = 512
    per_b = seq // tm
    return pl.pallas_call(
        partial(_inproj_kernel, tiles_per_seq=per_b),
        out_shape=(jax.ShapeDtypeStruct((t, N_MAIN), BF16), jax.ShapeDtypeStruct((t, N_GATE), F32)),
        grid=(t // tm,),
        in_specs=[pl.BlockSpec((tm, d), lambda i: (i, 0)),
                  pl.BlockSpec((1, 6, d), lambda i: (i // per_b, 0, 0)),
                  pl.BlockSpec((1, d), lambda i: (0, 0)),
                  pl.BlockSpec((d, N_ALL), lambda i: (0, 0))],
        out_specs=(pl.BlockSpec((tm, N_MAIN), lambda i: (i, 0)),
                   pl.BlockSpec((tm, N_GATE), lambda i: (i, 0))),
        compiler_params=_params(("arbitrary",)),
        name="in_proj",
    )(x2d, mod, g1, w_all)


def _compress_kernel(tok_ref, w1_ref, pos_ref, w2_ref, o_ref):
    tok = tok_ref[0, 0]
    half = CMP_STRIDE * HEAD_DIM
    first = _dot(tok, w1_ref[0, :half, :])
    second = _dot(tok, w1_ref[0, half:, :])
    posb = _dot(pos_ref[0], w1_ref[0].astype(F32), precision=lax.Precision.HIGHEST)
    hidden = first + jnp.concatenate([second[1:], second[:1]], axis=0) + posb
    o_ref[0, 0] = _dot(jax.nn.gelu(hidden).astype(BF16), w2_ref[0]).astype(BF16)


def _compress(tokr, w1, posflat, w2dup):
    bsz, _, nb, kdim = tokr.shape
    return pl.pallas_call(
        _compress_kernel,
        out_shape=jax.ShapeDtypeStruct((bsz, 4, nb, LANES), BF16),
        grid=(bsz, 4),
        in_specs=[pl.BlockSpec((1, 1, nb, kdim), lambda b, j: (b, j, 0, 0)),
                  pl.BlockSpec((1, 2 * kdim, CMP_HIDDEN), lambda b, j: (j // 2, 0, 0)),
                  pl.BlockSpec((1, 1, 2 * kdim), lambda b, j: (j // 2, 0, 0)),
                  pl.BlockSpec((1, CMP_HIDDEN, LANES), lambda b, j: (j // 2, 0, 0))],
        out_specs=pl.BlockSpec((1, 1, nb, LANES), lambda b, j: (b, j, 0, 0)),
        compiler_params=_params(("arbitrary", "arbitrary")),
        name="compress",
    )(tokr, w1, posflat, w2dup)


def _bias_kernel(tab_ref, bt_ref, bc_ref, ot_ref, oc_ref):
    h = pl.program_id(0)
    far = tab_ref[h * REL_BUCKETS + REL_BUCKETS - 1]

    def lookup(bkt, masked):
        acc = jnp.full(bkt.shape, masked, F32)
        for b in range(REL_BUCKETS):
            acc = jnp.where(bkt == b, tab_ref[h * REL_BUCKETS + b] - far, acc)
        return acc

    ot_ref[0] = lookup(bt_ref[...], NEG)
    oc_ref[0] = lookup(bc_ref[...], 0.0)


def _bias_tables(tab_flat, bkt_t, bkt_c):
    nt = bkt_t.shape[0]
    return pl.pallas_call(
        _bias_kernel,
        out_shape=(jax.ShapeDtypeStruct((N_ATTN_HEADS, nt, TQ, TQ), F32),
                   jax.ShapeDtypeStruct((N_ATTN_HEADS, TQ, CMP_NEAR), F32)),
        grid_spec=pltpu.PrefetchScalarGridSpec(
            num_scalar_prefetch=1,
            grid=(N_ATTN_HEADS,),
            in_specs=[pl.BlockSpec((nt, TQ, TQ), lambda h, tab: (0, 0, 0)),
                      pl.BlockSpec((TQ, CMP_NEAR), lambda h, tab: (0, 0))],
            out_specs=(pl.BlockSpec((1, nt, TQ, TQ), lambda h, tab: (h, 0, 0, 0)),
                       pl.BlockSpec((1, TQ, CMP_NEAR), lambda h, tab: (h, 0, 0))),
        ),
        compiler_params=_params(("arbitrary",)),
        name="bias_tables",
    )(tab_flat, bkt_t, bkt_c)


def _flash_reset(m_ref, l_ref, acc_ref):
    m_ref[...] = jnp.full(m_ref.shape, 3.0 * NEG, F32)
    l_ref[...] = jnp.zeros(l_ref.shape, F32)
    acc_ref[...] = jnp.zeros(acc_ref.shape, F32)


def _flash_step(q_all, k, v, bias, m_ref, l_ref, acc_ref):
    s = _dot_nt(q_all, k)
    if bias is not None:
        s = s + bias
    m_prev = m_ref[...]
    m_next = jnp.maximum(m_prev, jnp.max(s, axis=1, keepdims=True))
    alpha = jnp.exp(m_prev - m_next)
    p = jnp.exp(s - jnp.tile(m_next, (1, s.shape[1] // LANES)))
    l_ref[...] = alpha * l_ref[...] + jnp.sum(p, axis=1, keepdims=True)
    acc_ref[...] = alpha * acc_ref[...] + _dot(p.astype(BF16), v)
    m_ref[...] = m_next


def _rows(ref, start, size):
    return ref[pl.ds(pl.multiple_of(start, TQ), size), :]


def _causal_flash(q_all, k_ref, v_ref, qi, bias_near, m_ref, l_ref, acc_ref):
    _flash_reset(m_ref, l_ref, acc_ref)
    far_tiles = jnp.maximum(qi - 1, 0)
    per = TK_FAR // TQ

    def far_body(j, carry):
        _flash_step(q_all, _rows(k_ref, j * TK_FAR, TK_FAR), _rows(v_ref, j * TK_FAR, TK_FAR), None,
                    m_ref, l_ref, acc_ref)
        return carry

    lax.fori_loop(0, far_tiles // per, far_body, 0)

    @pl.when(far_tiles % per == 1)
    def _():
        start = (far_tiles - 1) * TQ
        _flash_step(q_all, _rows(k_ref, start, TQ), _rows(v_ref, start, TQ), None, m_ref, l_ref, acc_ref)

    @pl.when(qi >= 1)
    def _():
        _flash_step(q_all, _rows(k_ref, (qi - 1) * TQ, TQ), _rows(v_ref, (qi - 1) * TQ, TQ), bias_near(1),
                    m_ref, l_ref, acc_ref)

    _flash_step(q_all, _rows(k_ref, qi * TQ, TQ), _rows(v_ref, qi * TQ, TQ), bias_near(0), m_ref, l_ref, acc_ref)
    return acc_ref[...] / l_ref[...]


def _nsa_kernel(q_ref, kc_ref, vc_ref, ks_ref, vs_ref, kw_ref, vw_ref, gate_ref, bc_ref, bt_ref, cts_ref, place_ref,
                o_ref, m_ref, l_ref, acc_ref, *, n_top):
    qi = pl.program_id(2)
    rows = NSA_HPG * TQ
    q = q_ref[...]
    q_all = jnp.concatenate([q[:, h * LANES:(h + 1) * LANES] for h in range(NSA_HPG)], axis=0)

    kc = kc_ref[0, 0]
    nb = kc.shape[0]
    c0 = qi * (TQ // CMP_STRIDE)
    pr = lax.broadcasted_iota(jnp.int32, (CMP_NEAR, nb), 0)
    pc = lax.broadcasted_iota(jnp.int32, (CMP_NEAR, nb), 1)
    placement = jnp.where(pr + (c0 - CMP_NEAR // 2) == pc, 1.0, 0.0)
    bias_c = _dot(bc_ref[...].reshape(rows, CMP_NEAR), placement, precision=lax.Precision.HIGHEST)
    blk_c = lax.broadcasted_iota(jnp.int32, (rows, nb), 1)
    t_c = qi * TQ + lax.broadcasted_iota(jnp.int32, (rows, nb), 0) % TQ
    valid = (blk_c * CMP_STRIDE + (CMP_BLOCK - 1) <= t_c) & (blk_c < nb - 1)
    s_c = jnp.where(valid, _dot_nt(q_all, kc) + bias_c, NEG)
    m_c = jnp.max(s_c, axis=1, keepdims=True)
    p_c = jnp.where(valid, jnp.exp(s_c - m_c), 0.0)
    l_c = jnp.sum(p_c, axis=1, keepdims=True)
    p_c = p_c / jnp.where(l_c > 0.0, l_c, 1.0)
    o_c = _dot(p_c.astype(BF16), vc_ref[0, 0])

    p_sum = p_c[0:TQ] + p_c[TQ:2 * TQ] + p_c[2 * TQ:3 * TQ] + p_c[3 * TQ:4 * TQ]
    imp_t = _dot_nt(cts_ref[...], p_sum, precision=lax.Precision.HIGHEST)
    ns = imp_t.shape[0]
    blk_id = lax.broadcasted_iota(jnp.int32, (ns, TQ), 0)
    tpos = qi * TQ + lax.broadcasted_iota(jnp.int32, (ns, TQ), 1)
    cur = tpos // SEL_BLOCK
    forced = (blk_id == 0) | (blk_id == cur) | (blk_id == cur - 1)
    score = jnp.where(forced, FORCED_SCORE, jnp.where(blk_id <= cur, imp_t, NEG))
    rank = jnp.zeros((ns, TQ), F32)
    for k in range(ns):
        rk = score[k:k + 1, :]
        beats = (rk > score) | ((rk == score) & (blk_id > k))
        rank = rank + jnp.where(beats, 1.0, 0.0)
    sel = jnp.where(rank < float(n_top), 1.0, 0.0).T.astype(BF16)
    placed = _dot(sel, place_ref[...])
    lane = lax.broadcasted_iota(jnp.int32, (TQ, LANES), 1)
    mask_lanes = jnp.where(lane >= HEAD_DIM, (placed - 1.0) * (-NEG), 0.0).astype(BF16)
    q_sel = q_all + jnp.concatenate([mask_lanes] * NSA_HPG, axis=0)

    def bias_tile(idx):
        return bt_ref[:, idx].reshape(rows, TQ)

    o_s = _causal_flash(q_sel, ks_ref, vs_ref, qi, bias_tile, m_ref, l_ref, acc_ref)

    _flash_reset(m_ref, l_ref, acc_ref)

    @pl.when(qi >= 2)
    def _():
        _flash_step(q_all, _rows(kw_ref, (qi - 2) * TQ, TQ), _rows(vw_ref, (qi - 2) * TQ, TQ), bias_tile(2),
                    m_ref, l_ref, acc_ref)

    @pl.when(qi >= 1)
    def _():
        _flash_step(q_all, _rows(kw_ref, (qi - 1) * TQ, TQ), _rows(vw_ref, (qi - 1) * TQ, TQ), bias_tile(1),
                    m_ref, l_ref, acc_ref)

    _flash_step(q_all, _rows(kw_ref, qi * TQ, TQ), _rows(vw_ref, qi * TQ, TQ), bias_tile(0), m_ref, l_ref, acc_ref)
    o_w = acc_ref[...] / l_ref[...]

    gt = jax.nn.sigmoid(gate_ref[...])
    low = lane < HEAD_DIM
    outs = []
    for h in range(NSA_HPG):
        r = slice(h * TQ, (h + 1) * TQ)
        outs.append(gt[:, 3 * h:3 * h + 1] * o_c[r] + gt[:, 3 * h + 1:3 * h + 2] * o_s[r]
                    + gt[:, 3 * h + 2:3 * h + 3] * o_w[r])
    o_ref[:, 0:LANES] = jnp.where(low, outs[0], outs[1]).astype(BF16)
    o_ref[:, LANES:2 * LANES] = jnp.where(low, outs[2], outs[3]).astype(BF16)


def _nsa(main, kvcmp, gates, bias_c, bias_t, cts_t, place, bsz, seq):
    nq = seq // TQ
    nb = kvcmp.shape[2]
    ns = cts_t.shape[0]
    nt = bias_t.shape[1]
    kernel = partial(_nsa_kernel, n_top=min(SEL_TOPN, ns))
    col = lambda c: c // LANES
    return pl.pallas_call(
        kernel,
        out_shape=jax.ShapeDtypeStruct((bsz * seq, NSA_WIDTH), BF16),
        grid=(bsz, NSA_GROUPS, nq),
        in_specs=[
            pl.BlockSpec((TQ, NSA_HPG * LANES), lambda b, g, i: (b * nq + i, g)),
            pl.BlockSpec((1, 1, nb, LANES), lambda b, g, i: (b, g, 0, 0)),
            pl.BlockSpec((1, 1, nb, LANES), lambda b, g, i: (b, 2 + g, 0, 0)),
            pl.BlockSpec((seq, LANES), lambda b, g, i: (b, col(COL_KSEL) + g)),
            pl.BlockSpec((seq, LANES), lambda b, g, i: (b, col(COL_VSEL) + g)),
            pl.BlockSpec((seq, LANES), lambda b, g, i: (b, col(COL_KWIN) + g)),
            pl.BlockSpec((seq, LANES), lambda b, g, i: (b, col(COL_VWIN) + g)),
            pl.BlockSpec((TQ, LANES), lambda b, g, i: (b * nq + i, g)),
            pl.BlockSpec((NSA_HPG, TQ, CMP_NEAR), lambda b, g, i: (g, 0, 0)),
            pl.BlockSpec((NSA_HPG, nt, TQ, TQ), lambda b, g, i: (g, 0, 0, 0)),
            pl.BlockSpec((ns, nb), lambda b, g, i: (0, 0)),
            pl.BlockSpec((ns, LANES), lambda b, g, i: (0, 0)),
        ],
        out_specs=pl.BlockSpec((TQ, 2 * LANES), lambda b, g, i: (b * nq + i, g)),
        scratch_shapes=[pltpu.VMEM((NSA_HPG * TQ, LANES), F32),
                        pltpu.VMEM((NSA_HPG * TQ, LANES), F32),
                        pltpu.VMEM((NSA_HPG * TQ, LANES), F32)],
        compiler_params=_params(("arbitrary", "arbitrary", "arbitrary")),
        name="nsa_attention",
    )(main, kvcmp, kvcmp, main, main, main, main, gates, bias_c, bias_t, cts_t, place)


def _diff_kernel(q_ref, k_ref, v_ref, bt_ref, lam_ref, g_ref, o_ref, m_ref, l_ref, acc_ref):
    qi = pl.program_id(2)
    q = q_ref[...]
    q_all = jnp.concatenate([q[:, 0:LANES], q[:, LANES:2 * LANES]], axis=0)

    def bias_tile(idx):
        b = bt_ref[0, idx]
        return jnp.concatenate([b, b], axis=0)

    o = _causal_flash(q_all, k_ref, v_ref, qi, bias_tile, m_ref, l_ref, acc_ref)

    lam_rows = lam_ref[...]
    lam = (jnp.exp(jnp.sum(lam_rows[0:1] * lam_rows[1:2], axis=1, keepdims=True))
           - jnp.exp(jnp.sum(lam_rows[2:3] * lam_rows[3:4], axis=1, keepdims=True)) + LAM_INIT)
    od = o[0:TQ] - lam * o[TQ:2 * TQ]
    ms = jnp.mean(od * od, axis=-1, keepdims=True)
    o_ref[...] = (od * lax.rsqrt(ms + RMS_EPS) * g_ref[...] * (1.0 - LAM_INIT)).astype(BF16)


def _diff(main, bias_t, lam_rows, subln_g, bsz, seq):
    nq = seq // TQ
    return pl.pallas_call(
        _diff_kernel,
        out_shape=jax.ShapeDtypeStruct((bsz * seq, DIFF_WIDTH), BF16),
        grid=(bsz, DIFF_HEADS, nq),
        in_specs=[
            pl.BlockSpec((TQ, 2 * LANES), lambda b, h, i: (b * nq + i, COL_DQ // (2 * LANES) + h)),
            pl.BlockSpec((seq, LANES), lambda b, h, i: (b, COL_DK // LANES + h)),
            pl.BlockSpec((seq, LANES), lambda b, h, i: (b, COL_DV // LANES + h)),
            pl.BlockSpec((1, bias_t.shape[1], TQ, TQ), lambda b, h, i: (NSA_HEADS + h, 0, 0, 0)),
            pl.BlockSpec((8, LANES), lambda b, h, i: (0, 0)),
            pl.BlockSpec((1, LANES), lambda b, h, i: (0, 0)),
        ],
        out_specs=pl.BlockSpec((TQ, LANES), lambda b, h, i: (b * nq + i, h)),
        scratch_shapes=[pltpu.VMEM((2 * TQ, LANES), F32),
                        pltpu.VMEM((2 * TQ, LANES), F32),
                        pltpu.VMEM((2 * TQ, LANES), F32)],
        compiler_params=_params(("arbitrary", "arbitrary", "arbitrary")),
        name="diff_attention",
    )(main, main, main, bias_t, lam_rows, subln_g)


def _post_kernel(on_ref, od_ref, x_ref, mod_ref, g_ref, wo_ref, wq_ref, sk_ref, x1_ref, h2t_ref, subt_ref):
    mod = mod_ref[0]
    mixed = _dot(on_ref[...], wo_ref[0:NSA_WIDTH, :]) + _dot(od_ref[...], wo_ref[NSA_WIDTH:, :])
    x1 = x_ref[...] + mod[2:3] * mixed
    x1_ref[...] = x1
    h2 = _rms_mod(x1, g_ref[...], mod[4:5], mod[3:4])
    h2t_ref[...] = h2.T.astype(BF16)
    qp = _dot(h2.astype(BF16), wq_ref[...])
    for hp in range(2 * PEER_HEADS):
        qs = qp[:, hp * LANES:(hp + 1) * LANES]
        subt_ref[hp] = _dot_nt(sk_ref[hp % 2], qs, precision=lax.Precision.HIGHEST)


def _post(o_nsa, o_d, x2d, mod, g2, w_out, wq, sub_keys, seq):
    t, d = x2d.shape
    tm = 256
    per_b = seq // tm
    nq = wq.shape[1]
    return pl.pallas_call(
        _post_kernel,
        out_shape=(jax.ShapeDtypeStruct((t, d), F32),
                   jax.ShapeDtypeStruct((d, t), BF16),
                   jax.ShapeDtypeStruct((2 * PEER_HEADS, PEER_NKEYS, t), F32)),
        grid=(t // tm,),
        in_specs=[pl.BlockSpec((tm, NSA_WIDTH), lambda i: (i, 0)),
                  pl.BlockSpec((tm, DIFF_WIDTH), lambda i: (i, 0)),
                  pl.BlockSpec((tm, d), lambda i: (i, 0)),
                  pl.BlockSpec((1, 6, d), lambda i: (i // per_b, 0, 0)),
                  pl.BlockSpec((1, d), lambda i: (0, 0)),
                  pl.BlockSpec((d, d), lambda i: (0, 0)),
                  pl.BlockSpec((d, nq), lambda i: (0, 0)),
                  pl.BlockSpec((2, PEER_NKEYS, LANES), lambda i: (0, 0, 0))],
        out_specs=(pl.BlockSpec((tm, d), lambda i: (i, 0)),
                   pl.BlockSpec((d, tm), lambda i: (0, i)),
                   pl.BlockSpec((2 * PEER_HEADS, PEER_NKEYS, tm), lambda i: (0, 0, i))),
        compiler_params=_params(("arbitrary",)),
        name="post_attention",
    )(o_nsa, o_d, x2d, mod, g2, w_out, wq, sub_keys)


_PEER_PAIRS = [(r, c) for r in range(PEER_TOPK + 1) for c in range(PEER_TOPK + 1)
               if (r + 1) * (c + 1) <= PEER_TOPK + 1]


def _top_rows(x, n):
    out = []
    cur = x
    for _ in range(n):
        m = jnp.max(cur, axis=0, keepdims=True)
        out.append(m)
        cur = jnp.where(cur == m, -jnp.inf, cur)
    return out


LOG2E = 1.4426950408889634


def _peer_select_kernel(sub_ref, cnt_ref, ea_ref, rank_ref, eb_ref, cand_ref):
    cand_ref[...] = jnp.full(cand_ref.shape, -jnp.inf, F32)
    for h in range(PEER_HEADS):
        a = sub_ref[2 * h]
        b = sub_ref[2 * h + 1]
        ta = _top_rows(a, PEER_TOPK + 1)
        tb = _top_rows(b, PEER_TOPK + 1)
        for n, (r, c) in enumerate(_PEER_PAIRS):
            cand_ref[n:n + 1, :] = ta[r] + tb[c]
        cand = cand_ref[...]
        top = _top_rows(cand, PEER_TOPK + 1)
        mid = 0.5 * (top[PEER_TOPK - 1] + top[PEER_TOPK])
        best = ta[0] + tb[0]
        z = jnp.sum(jnp.where(cand > mid, jnp.exp(cand - best), 0.0), axis=0, keepdims=True)
        rank = jnp.zeros(b.shape, F32)
        cnt = jnp.zeros(a.shape, F32)
        for c in range(PEER_TOPK + 1):
            rank = rank + jnp.where(b < tb[c], 1.0, 0.0)
            cnt = cnt + jnp.where(a + tb[c] > mid, 1.0, 0.0)
        rank_ref[h] = rank.astype(BF16)
        cnt_ref[h] = cnt
        eb_ref[h] = jnp.exp(b - tb[0]).astype(BF16)
        ea_ref[h] = (0.5 * jnp.exp(a - ta[0])) / z


def _peer_select(subt):
    nhp, nk, t = subt.shape
    tt = 256
    wide = jax.ShapeDtypeStruct((PEER_HEADS, nk, t), F32)
    narrow = jax.ShapeDtypeStruct((PEER_HEADS, nk, t), BF16)
    spec = pl.BlockSpec((PEER_HEADS, nk, tt), lambda i: (0, 0, i))
    return pl.pallas_call(
        _peer_select_kernel,
        out_shape=(wide, wide, narrow, narrow),
        grid=(t // tt,),
        in_specs=[pl.BlockSpec((nhp, nk, tt), lambda i: (0, 0, i))],
        out_specs=(spec, spec, spec, spec),
        scratch_shapes=[pltpu.VMEM((-(-len(_PEER_PAIRS) // 8) * 8, tt), F32)],
        compiler_params=_params(("arbitrary",)),
        name="peer_select",
    )(subt)


def _peer_dense_kernel(h2t_ref, u_ref, vt_ref, cnt_ref, ea_ref, rank_ref, eb_ref, x1_ref, mod_ref, g_ref,
                       o_ref, acc_ref, y_ref, pre_ref, *, rows_per_step):
    j = pl.program_id(1)

    @pl.when(j == 0)
    def _():
        acc_ref[...] = jnp.zeros(acc_ref.shape, F32)

    tt = h2t_ref.shape[1]
    zero = jnp.zeros((PEER_NKEYS, LANES), BF16)
    c1 = math.sqrt(2.0 / math.pi)
    c2 = c1 * 0.044715
    pre_ref[...] = _dot(u_ref[...], h2t_ref[...])
    for ii in range(rows_per_step):
        r = slice(ii * PEER_NKEYS, (ii + 1) * PEER_NKEYS)
        for lt in range(tt // LANES):
            c = slice(lt * LANES, (lt + 1) * LANES)
            w = None
            for h in range(PEER_HEADS):
                keep = rank_ref[h, :, c] < cnt_ref[h, ii:ii + 1, c].astype(BF16)
                term = jnp.where(keep, eb_ref[h, :, c], zero) * ea_ref[h, ii:ii + 1, c].astype(BF16)
                w = term if w is None else w + term
            x = pre_ref[r, c]
            y_ref[r, c] = w * (x * (1.0 + jnp.tanh(x * (c1 + c2 * (x * x))))).astype(BF16)
    acc_ref[...] += _dot(vt_ref[...], y_ref[...])

    @pl.when(j == pl.num_programs(1) - 1)
    def _():
        mod = mod_ref[0]
        x2 = x1_ref[...] + mod[5:6] * acc_ref[...].T
        ms = jnp.mean(x2 * x2, axis=-1, keepdims=True)
        o_ref[...] = x2 * lax.rsqrt(ms + RMS_EPS) * g_ref[...]


def _peer_dense(h2t, u_b, vt_b, cnt, ea, rank, eb, x1, mod, final_g, seq):
    d, t = h2t.shape
    ne = u_b.shape[0]
    tt = 512
    ec = 1024
    rows = ec // PEER_NKEYS
    per_b = seq // tt
    nk = rank.shape[1]
    kernel = partial(_peer_dense_kernel, rows_per_step=rows)
    return pl.pallas_call(
        kernel,
        out_shape=jax.ShapeDtypeStruct((t, d), F32),
        grid=(t // tt, ne // ec),
        in_specs=[pl.BlockSpec((d, tt), lambda i, j: (0, i)),
                  pl.BlockSpec((ec, d), lambda i, j: (j, 0)),
                  pl.BlockSpec((d, ec), lambda i, j: (0, j)),
                  pl.BlockSpec((PEER_HEADS, rows, tt), lambda i, j: (0, j, i)),
                  pl.BlockSpec((PEER_HEADS, rows, tt), lambda i, j: (0, j, i)),
                  pl.BlockSpec((PEER_HEADS, nk, tt), lambda i, j: (0, 0, i)),
                  pl.BlockSpec((PEER_HEADS, nk, tt), lambda i, j: (0, 0, i)),
                  pl.BlockSpec((tt, d), lambda i, j: (i, 0)),
                  pl.BlockSpec((1, 6, d), lambda i, j: (i // per_b, 0, 0)),
                  pl.BlockSpec((1, d), lambda i, j: (0, 0))],
        out_specs=pl.BlockSpec((tt, d), lambda i, j: (i, 0)),
        scratch_shapes=[pltpu.VMEM((d, tt), F32), pltpu.VMEM((ec, tt), BF16), pltpu.VMEM((ec, tt), F32)],
        compiler_params=_params(("arbitrary", "arbitrary")),
        name="peer_dense",
    )(h2t, u_b, vt_b, cnt, ea, rank, eb, x1, mod, final_g)


def _t5_bucket_np(dist):
    n = np.maximum(dist, 0)
    max_exact = REL_BUCKETS // 2
    nf = np.maximum(n, 1).astype(np.float32)
    large = max_exact + (np.log(nf / np.float32(max_exact)) / np.float32(math.log(REL_MAX_DIST / max_exact))
                         * np.float32(REL_BUCKETS - max_exact)).astype(np.int32)
    return np.where(n < max_exact, n, np.minimum(large, REL_BUCKETS - 1)).astype(np.int32)


def _token_bucket_tiles():
    i = np.arange(TQ)[:, None]
    j = np.arange(TQ)[None, :]
    d0 = i - j
    d1 = TQ + i - j
    d2 = WINDOW + i - j
    assert WINDOW == 2 * TQ and _t5_bucket_np(np.array([TQ + 1]))[0] == REL_BUCKETS - 1
    bkt = _t5_bucket_np(np.stack([d0, d1, d2]))
    valid = np.stack([d0 >= 0, np.ones_like(d0, bool), d2 < WINDOW])
    return np.where(valid, bkt, -1).astype(np.int32)


def _cmp_bucket_tile():
    i = np.arange(TQ)[:, None]
    c = np.arange(CMP_NEAR)[None, :] - CMP_NEAR // 2
    d = i - (c * CMP_STRIDE + CMP_BLOCK - 1)
    assert _t5_bucket_np(np.array([(CMP_NEAR // 2 + 1) * CMP_STRIDE - CMP_BLOCK + 1]))[0] == REL_BUCKETS - 1
    assert (CMP_NEAR // 2) * CMP_STRIDE + CMP_BLOCK - 1 > TQ - 1
    return np.where(d >= 0, _t5_bucket_np(d), REL_BUCKETS - 1).astype(np.int32)


def _block_overlap_t(nb, n_sel):
    cmp_start = np.arange(nb) * CMP_STRIDE
    sel_start = np.arange(n_sel) * SEL_BLOCK
    lo = np.maximum(cmp_start[None, :], sel_start[:, None])
    hi = np.minimum(cmp_start[None, :] + CMP_BLOCK, sel_start[:, None] + SEL_BLOCK)
    ov = np.maximum(hi - lo, 0).astype(np.float32) / CMP_BLOCK
    ov[:, nb - 1] = 0.0
    return ov


def _arrange_w_in(w):
    d = w.shape[0]
    scale = HEAD_DIM ** -0.5
    o_kv = NSA_WIDTH
    o_gate = o_kv + 3 * 2 * NSA_GROUPS * HEAD_DIM
    o_dq = o_gate + NSA_HEADS * 3
    o_dk = o_dq + DIFF_WIDTH
    o_dv = o_dk + DIFF_WIDTH
    z64 = lambda *lead: jnp.zeros((d,) + lead + (HEAD_DIM,), w.dtype)
    qn = (w[:, :o_kv] * scale).reshape(d, NSA_HEADS, HEAD_DIM)
    qn = jnp.concatenate([qn, z64(NSA_HEADS)], axis=-1).reshape(d, NSA_HEADS * LANES)
    kv = w[:, o_kv:o_gate].reshape(d, 3, 2, NSA_GROUPS, HEAD_DIM)
    gates = w[:, o_gate:o_dq].reshape(d, NSA_GROUPS, NSA_HPG * 3)
    dq = (w[:, o_dq:o_dk] * scale).reshape(d, DIFF_HEADS, 2, HEAD_DIM)
    dq = jnp.concatenate([dq[:, :, 0], z64(DIFF_HEADS), z64(DIFF_HEADS), dq[:, :, 1]], axis=-1)
    dq = dq.reshape(d, DIFF_HEADS * 2 * LANES)
    dk = w[:, o_dk:o_dv]
    dv = w[:, o_dv:]
    kvc = kv[:, 0].reshape(d, 2 * NSA_GROUPS * HEAD_DIM)
    kpad = lambda k: jnp.concatenate([k, z64(NSA_GROUPS)], axis=-1).reshape(d, NSA_GROUPS * LANES)
    vdup = lambda v: jnp.concatenate([v, v], axis=-1).reshape(d, NSA_GROUPS * LANES)
    gpad = jnp.pad(gates, ((0, 0), (0, 0), (0, LANES - NSA_HPG * 3))).reshape(d, N_GATE)
    cols = [qn, dq, dk, dv, kvc, kpad(kv[:, 1, 0]), kpad(kv[:, 2, 0]), vdup(kv[:, 1, 1]), vdup(kv[:, 2, 1]), gpad]
    return jnp.concatenate(cols, axis=1).astype(BF16)


def kernel(x, c, rel_bias, ada_w, ada_b, norm1_g, norm2_g, w_in, w_out, cmp_pos, cmp_w1, cmp_w2, lam_q1, lam_k1,
           lam_q2, lam_k2, diff_subln_g, peer_wq, peer_sub_keys, peer_u, peer_v, final_g):
    bsz, seq, d = x.shape
    t = bsz * seq
    nb = seq // CMP_STRIDE
    n_sel = seq // SEL_BLOCK
    assert seq % TK_FAR == 0 and n_sel <= HEAD_DIM
    x2d = x.reshape(t, d)

    mod = _modulation(c, ada_w[0], ada_b[0]).reshape(bsz, 6, d)
    main, gates = _in_proj(x2d, mod, norm1_g[0].reshape(1, d), _arrange_w_in(w_in[0]), seq)

    tokr = (main[:, COL_KVC:COL_KVC + 4 * HEAD_DIM].reshape(bsz, nb, CMP_STRIDE, 4, HEAD_DIM)
            .transpose(0, 3, 1, 2, 4).reshape(bsz, 4, nb, CMP_STRIDE * HEAD_DIM))
    w2dup = jnp.concatenate([cmp_w2[0], cmp_w2[0]], axis=-1).astype(BF16)
    kvcmp = _compress(tokr, cmp_w1[0].astype(BF16), cmp_pos[0].reshape(2, 1, CMP_BLOCK * HEAD_DIM), w2dup)

    bias_t, bias_c = _bias_tables(rel_bias.astype(F32).T.reshape(-1), jnp.asarray(_token_bucket_tiles()),
                                  jnp.asarray(_cmp_bucket_tile()))
    cts_t = jnp.asarray(_block_overlap_t(nb, n_sel))
    place = jnp.asarray(np.eye(n_sel, LANES, HEAD_DIM, dtype=np.float32), BF16)

    o_nsa = _nsa(main, kvcmp, gates, bias_c, bias_t, cts_t, place, bsz, seq)

    lam_rows = jnp.pad(jnp.stack([lam_q1[0], lam_k1[0], lam_q2[0], lam_k2[0]]).astype(F32),
                       ((0, 4), (0, LANES - lam_q1.shape[1])))
    o_d = _diff(main, bias_t, lam_rows, diff_subln_g[0].reshape(1, DIFF_V_DIM), bsz, seq)

    x1, h2t, subt = _post(o_nsa, o_d, x2d, mod, norm2_g[0].reshape(1, d), w_out[0].astype(BF16),
                          peer_wq[0].astype(BF16), peer_sub_keys[0], seq)
    cnt, ea, rank, eb = _peer_select(subt)
    out = _peer_dense(h2t, peer_u[0].astype(BF16), peer_v[0].T.astype(BF16), cnt, ea, rank, eb, x1, mod,
                      final_g.reshape(1, d), seq)
    return out.reshape(bsz, seq, d)
```

```python
import math
from functools import partial

import numpy as np
import jax
import jax.numpy as jnp
from jax import lax
from jax.experimental import pallas as pl
from jax.experimental.pallas import tpu as pltpu

F32 = jnp.float32
BF16 = jnp.bfloat16

D_MODEL = 1024
HEAD_DIM = 64
NSA_GROUPS = 2
NSA_HPG = 4
NSA_HEADS = NSA_GROUPS * NSA_HPG
NSA_WIDTH = NSA_HEADS * HEAD_DIM
CMP_BLOCK = 32
CMP_STRIDE = 16
CMP_HIDDEN = 256
SEL_BLOCK = 64
SEL_TOPN = 16
WINDOW = 512
DIFF_HEADS = 4
DIFF_V_DIM = 128
DIFF_WIDTH = DIFF_HEADS * DIFF_V_DIM
N_ATTN_HEADS = NSA_HEADS + DIFF_HEADS
REL_BUCKETS = 32
REL_MAX_DIST = 128
PEER_HEADS = 8
PEER_NKEYS = 128
PEER_EXPERTS = PEER_NKEYS ** 2
PEER_TOPK = 16
RMS_EPS = 1e-6
NEG = -1e30
FORCED_SCORE = 1e9
LAM_INIT = 0.8 - 0.6 * math.exp(-0.3 * 0)

LANES = 128
TQ = 256
TK_FAR = 512
CMP_NEAR = 32
VMEM_LIMIT = 56 * 1024 * 1024

COL_QN = 0
COL_DQ = 1024
COL_DK = 2048
COL_DV = 2560
COL_KVC = 3072
COL_KSEL = 3328
COL_KWIN = 3584
COL_VSEL = 3840
COL_VWIN = 4096
N_MAIN = 4352
N_GATE = 256
N_ALL = N_MAIN + N_GATE


def _dot(a, b, precision=None):
    return jnp.dot(a, b, preferred_element_type=F32, precision=precision)


def _dot_nt(a, b, precision=None):
    return lax.dot_general(a, b, (((1,), (1,)), ((), ())), preferred_element_type=F32, precision=precision)


def _params(sem):
    return pltpu.CompilerParams(dimension_semantics=sem, vmem_limit_bytes=VMEM_LIMIT)


def _mod_kernel(c_ref, w_ref, b_ref, o_ref):
    c = c_ref[...]
    cond = c * jax.nn.sigmoid(c)
    o_ref[...] = _dot(cond, w_ref[...], precision=lax.Precision.HIGHEST) + b_ref[...]


def _modulation(c, ada_w, ada_b):
    bsz, d = c.shape
    n = ada_w.shape[1]
    tn = 512
    return pl.pallas_call(
        _mod_kernel,
        out_shape=jax.ShapeDtypeStruct((bsz, n), F32),
        grid=(n // tn,),
        in_specs=[pl.BlockSpec((bsz, d), lambda j: (0, 0)),
                  pl.BlockSpec((d, tn), lambda j: (0, j)),
                  pl.BlockSpec((1, tn), lambda j: (0, j))],
        out_specs=pl.BlockSpec((bsz, tn), lambda j: (0, j)),
        compiler_params=_params(("arbitrary",)),
        name="modulation",
    )(c, ada_w, ada_b.reshape(1, n))


def _rms_mod(x, g, scale, shift):
    ms = jnp.mean(x * x, axis=-1, keepdims=True)
    y = x * lax.rsqrt(ms + RMS_EPS) * g
    return y * (1.0 + scale) + shift


def _inproj_kernel(x_ref, mod_ref, g_ref, w_ref, main_ref, gate_ref, *, tiles_per_seq):
    mod = mod_ref[0]
    h = _rms_mod(x_ref[...], g_ref[...], mod[1:2], mod[0:1]).astype(BF16)
    tm = h.shape[0]
    step = 512
    for c0 in range(0, N_MAIN, step):
        c1 = min(c0 + step, N_MAIN)
        main_ref[:, c0:c1] = _dot(h, w_ref[:, c0:c1]).astype(BF16)
    gate_ref[...] = _dot(h, w_ref[:, N_MAIN:N_ALL])
    pos = (pl.program_id(0) % tiles_per_seq) * tm + lax.broadcasted_iota(jnp.int32, (tm, LANES), 0)
    lane = lax.broadcasted_iota(jnp.int32, (tm, LANES), 1)
    onehot = jnp.where(lane == HEAD_DIM + pos // SEL_BLOCK, 1.0, 0.0).astype(BF16)
    for g in range(NSA_GROUPS):
        c0 = COL_KSEL + g * LANES
        main_ref[:, c0:c0 + LANES] = main_ref[:, c0:c0 + LANES] + onehot


def _in_proj(x2d, mod, g1, w_all, seq):
    t, d = x2d.shape
    tm = 512
    per_b = seq // tm
    return pl.pallas_call(
        partial(_inproj_kernel, tiles_per_seq=per_b),
        out_shape=(jax.ShapeDtypeStruct((t, N_MAIN), BF16), jax.ShapeDtypeStruct((t, N_GATE), F32)),
        grid=(t // tm,),
        in_specs=[pl.BlockSpec((tm, d), lambda i: (i, 0)),
                  pl.BlockSpec((1, 6, d), lambda i: (i // per_b, 0, 0)),
                  pl.BlockSpec((1, d), lambda i: (0, 0)),
                  pl.BlockSpec((d, N_ALL), lambda i: (0, 0))],
        out_specs=(pl.BlockSpec((tm, N_MAIN), lambda i: (i, 0)),
                   pl.BlockSpec((tm, N_GATE), lambda i: (i, 0))),
        compiler_params=_params(("arbitrary",)),
        name="in_proj",
    )(x2d, mod, g1, w_all)


def _compress_kernel(tok_ref, w1_ref, pos_ref, w2_ref, o_ref):
    tok = tok_ref[0, 0]
    half = CMP_STRIDE * HEAD_DIM
    first = _dot(tok, w1_ref[0, :half, :])
    second = _dot(tok, w1_ref[0, half:, :])
    posb = _dot(pos_ref[0], w1_ref[0].astype(F32), precision=lax.Precision.HIGHEST)
    hidden = first + jnp.concatenate([second[1:], second[:1]], axis=0) + posb
    o_ref[0, 0] = _dot(jax.nn.gelu(hidden).astype(BF16), w2_ref[0]).astype(BF16)


def _compress(tokr, w1, posflat, w2dup):
    bsz, _, nb, kdim = tokr.shape
    return pl.pallas_call(
        _compress_kernel,
        out_shape=jax.ShapeDtypeStruct((bsz, 4, nb, LANES), BF16),
        grid=(bsz, 4),
        in_specs=[pl.BlockSpec((1, 1, nb, kdim), lambda b, j: (b, j, 0, 0)),
                  pl.BlockSpec((1, 2 * kdim, CMP_HIDDEN), lambda b, j: (j // 2, 0, 0)),
                  pl.BlockSpec((1, 1, 2 * kdim), lambda b, j: (j // 2, 0, 0)),
                  pl.BlockSpec((1, CMP_HIDDEN, LANES), lambda b, j: (j // 2, 0, 0))],
        out_specs=pl.BlockSpec((1, 1, nb, LANES), lambda b, j: (b, j, 0, 0)),
        compiler_params=_params(("arbitrary", "arbitrary")),
        name="compress",
    )(tokr, w1, posflat, w2dup)


def _bias_kernel(tab_ref, bt_ref, bc_ref, ot_ref, oc_ref):
    h = pl.program_id(0)
    far = tab_ref[h * REL_BUCKETS + REL_BUCKETS - 1]

    def lookup(bkt, masked):
        acc = jnp.full(bkt.shape, masked, F32)
        for b in range(REL_BUCKETS):
            acc = jnp.where(bkt == b, tab_ref[h * REL_BUCKETS + b] - far, acc)
        return acc

    ot_ref[0] = lookup(bt_ref[...], NEG)
    oc_ref[0] = lookup(bc_ref[...], 0.0)


def _bias_tables(tab_flat, bkt_t, bkt_c):
    nt = bkt_t.shape[0]
    return pl.pallas_call(
        _bias_kernel,
        out_shape=(jax.ShapeDtypeStruct((N_ATTN_HEADS, nt, TQ, TQ), F32),
                   jax.ShapeDtypeStruct((N_ATTN_HEADS, TQ, CMP_NEAR), F32)),
        grid_spec=pltpu.PrefetchScalarGridSpec(
            num_scalar_prefetch=1,
            grid=(N_ATTN_HEADS,),
            in_specs=[pl.BlockSpec((nt, TQ, TQ), lambda h, tab: (0, 0, 0)),
                      pl.BlockSpec((TQ, CMP_NEAR), lambda h, tab: (0, 0))],
            out_specs=(pl.BlockSpec((1, nt, TQ, TQ), lambda h, tab: (h, 0, 0, 0)),
                       pl.BlockSpec((1, TQ, CMP_NEAR), lambda h, tab: (h, 0, 0))),
        ),
        compiler_params=_params(("arbitrary",)),
        name="bias_tables",
    )(tab_flat, bkt_t, bkt_c)


def _flash_reset(m_ref, l_ref, acc_ref):
    m_ref[...] = jnp.full(m_ref.shape, 3.0 * NEG, F32)
    l_ref[...] = jnp.zeros(l_ref.shape, F32)
    acc_ref[...] = jnp.zeros(acc_ref.shape, F32)


def _flash_step(q_all, k, v, bias, m_ref, l_ref, acc_ref):
    s = _dot_nt(q_all, k)
    if bias is not None:
        s = s + bias
    m_prev = m_ref[...]
    m_next = jnp.maximum(m_prev, jnp.max(s, axis=1, keepdims=True))
    alpha = jnp.exp(m_prev - m_next)
    p = jnp.exp(s - jnp.tile(m_next, (1, s.shape[1] // LANES)))
    l_ref[...] = alpha * l_ref[...] + jnp.sum(p, axis=1, keepdims=True)
    acc_ref[...] = alpha * acc_ref[...] + _dot(p.astype(BF16), v)
    m_ref[...] = m_next


def _rows(ref, start, size):
    return ref[pl.ds(pl.multiple_of(start, TQ), size), :]


def _causal_flash(q_all, k_ref, v_ref, qi, bias_near, m_ref, l_ref, acc_ref):
    _flash_reset(m_ref, l_ref, acc_ref)
    far_tiles = jnp.maximum(qi - 1, 0)
    per = TK_FAR // TQ

    def far_body(j, carry):
        _flash_step(q_all, _rows(k_ref, j * TK_FAR, TK_FAR), _rows(v_ref, j * TK_FAR, TK_FAR), None,
                    m_ref, l_ref, acc_ref)
        return carry

    lax.fori_loop(0, far_tiles // per, far_body, 0)

    @pl.when(far_tiles % per == 1)
    def _():
        start = (far_tiles - 1) * TQ
        _flash_step(q_all, _rows(k_ref, start, TQ), _rows(v_ref, start, TQ), None, m_ref, l_ref, acc_ref)

    @pl.when(qi >= 1)
    def _():
        _flash_step(q_all, _rows(k_ref, (qi - 1) * TQ, TQ), _rows(v_ref, (qi - 1) * TQ, TQ), bias_near(1),
                    m_ref, l_ref, acc_ref)

    _flash_step(q_all, _rows(k_ref, qi * TQ, TQ), _rows(v_ref, qi * TQ, TQ), bias_near(0), m_ref, l_ref, acc_ref)
    return acc_ref[...] / l_ref[...]


def _nsa_kernel(q_ref, kc_ref, vc_ref, ks_ref, vs_ref, kw_ref, vw_ref, gate_ref, bc_ref, bt_ref, cts_ref, place_ref,
                o_ref, m_ref, l_ref, acc_ref, *, n_top):
    qi = pl.program_id(2)
    rows = NSA_HPG * TQ
    q = q_ref[...]
    q_all = jnp.concatenate([q[:, h * LANES:(h + 1) * LANES] for h in range(NSA_HPG)], axis=0)

    kc = kc_ref[0, 0]
    nb = kc.shape[0]
    c0 = qi * (TQ // CMP_STRIDE)
    pr = lax.broadcasted_iota(jnp.int32, (CMP_NEAR, nb), 0)
    pc = lax.broadcasted_iota(jnp.int32, (CMP_NEAR, nb), 1)
    placement = jnp.where(pr + (c0 - CMP_NEAR // 2) == pc, 1.0, 0.0)
    bias_c = _dot(bc_ref[...].reshape(rows, CMP_NEAR), placement, precision=lax.Precision.HIGHEST)
    blk_c = lax.broadcasted_iota(jnp.int32, (rows, nb), 1)
    t_c = qi * TQ + lax.broadcasted_iota(jnp.int32, (rows, nb), 0) % TQ
    valid = (blk_c * CMP_STRIDE + (CMP_BLOCK - 1) <= t_c) & (blk_c < nb - 1)
    s_c = jnp.where(valid, _dot_nt(q_all, kc) + bias_c, NEG)
    m_c = jnp.max(s_c, axis=1, keepdims=True)
    p_c = jnp.where(valid, jnp.exp(s_c - m_c), 0.0)
    l_c = jnp.sum(p_c, axis=1, keepdims=True)
    p_c = p_c / jnp.where(l_c > 0.0, l_c, 1.0)
    o_c = _dot(p_c.astype(BF16), vc_ref[0, 0])

    p_sum = p_c[0:TQ] + p_c[TQ:2 * TQ] + p_c[2 * TQ:3 * TQ] + p_c[3 * TQ:4 * TQ]
    imp_t = _dot_nt(cts_ref[...], p_sum, precision=lax.Precision.HIGHEST)
    ns = imp_t.shape[0]
    blk_id = lax.broadcasted_iota(jnp.int32, (ns, TQ), 0)
    tpos = qi * TQ + lax.broadcasted_iota(jnp.int32, (ns, TQ), 1)
    cur = tpos // SEL_BLOCK
    forced = (blk_id == 0) | (blk_id == cur) | (blk_id == cur - 1)
    score = jnp.where(forced, FORCED_SCORE, jnp.where(blk_id <= cur, imp_t, NEG))
    rank = jnp.zeros((ns, TQ), F32)
    for k in range(ns):
        rk = score[k:k + 1, :]
        beats = (rk > score) | ((rk == score) & (blk_id > k))
        rank = rank + jnp.where(beats, 1.0, 0.0)
    sel = jnp.where(rank < float(n_top), 1.0, 0.0).T.astype(BF16)
    placed = _dot(sel, place_ref[...])
    lane = lax.broadcasted_iota(jnp.int32, (TQ, LANES), 1)
    mask_lanes = jnp.where(lane >= HEAD_DIM, (placed - 1.0) * (-NEG), 0.0).astype(BF16)
    q_sel = q_all + jnp.concatenate([mask_lanes] * NSA_HPG, axis=0)

    def bias_tile(idx):
        return bt_ref[:, idx].reshape(rows, TQ)

    o_s = _causal_flash(q_sel, ks_ref, vs_ref, qi, bias_tile, m_ref, l_ref, acc_ref)

    _flash_reset(m_ref, l_ref, acc_ref)

    @pl.when(qi >= 2)
    def _():
        _flash_step(q_all, _rows(kw_ref, (qi - 2) * TQ, TQ), _rows(vw_ref, (qi - 2) * TQ, TQ), bias_tile(2),
                    m_ref, l_ref, acc_ref)

    @pl.when(qi >= 1)
    def _():
        _flash_step(q_all, _rows(kw_ref, (qi - 1) * TQ, TQ), _rows(vw_ref, (qi - 1) * TQ, TQ), bias_tile(1),
                    m_ref, l_ref, acc_ref)

    _flash_step(q_all, _rows(kw_ref, qi * TQ, TQ), _rows(vw_ref, qi * TQ, TQ), bias_tile(0), m_ref, l_ref, acc_ref)
    o_w = acc_ref[...] / l_ref[...]

    gt = jax.nn.sigmoid(gate_ref[...])
    low = lane < HEAD_DIM
    outs = []
    for h in range(NSA_HPG):
        r = slice(h * TQ, (h + 1) * TQ)
        outs.append(gt[:, 3 * h:3 * h + 1] * o_c[r] + gt[:, 3 * h + 1:3 * h + 2] * o_s[r]
                    + gt[:, 3 * h + 2:3 * h + 3] * o_w[r])
    o_ref[:, 0:LANES] = jnp.where(low, outs[0], outs[1]).astype(BF16)
    o_ref[:, LANES:2 * LANES] = jnp.where(low, outs[2], outs[3]).astype(BF16)


def _nsa(main, kvcmp, gates, bias_c, bias_t, cts_t, place, bsz, seq):
    nq = seq // TQ
    nb = kvcmp.shape[2]
    ns = cts_t.shape[0]
    nt = bias_t.shape[1]
    kernel = partial(_nsa_kernel, n_top=min(SEL_TOPN, ns))
    col = lambda c: c // LANES
    return pl.pallas_call(
        kernel,
        out_shape=jax.ShapeDtypeStruct((bsz * seq, NSA_WIDTH), BF16),
        grid=(bsz, NSA_GROUPS, nq),
        in_specs=[
            pl.BlockSpec((TQ, NSA_HPG * LANES), lambda b, g, i: (b * nq + i, g)),
            pl.BlockSpec((1, 1, nb, LANES), lambda b, g, i: (b, g, 0, 0)),
            pl.BlockSpec((1, 1, nb, LANES), lambda b, g, i: (b, 2 + g, 0, 0)),
            pl.BlockSpec((seq, LANES), lambda b, g, i: (b, col(COL_KSEL) + g)),
            pl.BlockSpec((seq, LANES), lambda b, g, i: (b, col(COL_VSEL) + g)),
            pl.BlockSpec((seq, LANES), lambda b, g, i: (b, col(COL_KWIN) + g)),
            pl.BlockSpec((seq, LANES), lambda b, g, i: (b, col(COL_VWIN) + g)),
            pl.BlockSpec((TQ, LANES), lambda b, g, i: (b * nq + i, g)),
            pl.BlockSpec((NSA_HPG, TQ, CMP_NEAR), lambda b, g, i: (g, 0, 0)),
            pl.BlockSpec((NSA_HPG, nt, TQ, TQ), lambda b, g, i: (g, 0, 0, 0)),
            pl.BlockSpec((ns, nb), lambda b, g, i: (0, 0)),
            pl.BlockSpec((ns, LANES), lambda b, g, i: (0, 0)),
        ],
        out_specs=pl.BlockSpec((TQ, 2 * LANES), lambda b, g, i: (b * nq + i, g)),
        scratch_shapes=[pltpu.VMEM((NSA_HPG * TQ, LANES), F32),
                        pltpu.VMEM((NSA_HPG * TQ, LANES), F32),
                        pltpu.VMEM((NSA_HPG * TQ, LANES), F32)],
        compiler_params=_params(("arbitrary", "arbitrary", "arbitrary")),
        name="nsa_attention",
    )(main, kvcmp, kvcmp, main, main, main, main, gates, bias_c, bias_t, cts_t, place)


def _diff_kernel(q_ref, k_ref, v_ref, bt_ref, lam_ref, g_ref, o_ref, m_ref, l_ref, acc_ref):
    qi = pl.program_id(2)
    q = q_ref[...]
    q_all = jnp.concatenate([q[:, 0:LANES], q[:, LANES:2 * LANES]], axis=0)

    def bias_tile(idx):
        b = bt_ref[0, idx]
        return jnp.concatenate([b, b], axis=0)

    o = _causal_flash(q_all, k_ref, v_ref, qi, bias_tile, m_ref, l_ref, acc_ref)

    lam_rows = lam_ref[...]
    lam = (jnp.exp(jnp.sum(lam_rows[0:1] * lam_rows[1:2], axis=1, keepdims=True))
           - jnp.exp(jnp.sum(lam_rows[2:3] * lam_rows[3:4], axis=1, keepdims=True)) + LAM_INIT)
    od = o[0:TQ] - lam * o[TQ:2 * TQ]
    ms = jnp.mean(od * od, axis=-1, keepdims=True)
    o_ref[...] = (od * lax.rsqrt(ms + RMS_EPS) * g_ref[...] * (1.0 - LAM_INIT)).astype(BF16)


def _diff(main, bias_t, lam_rows, subln_g, bsz, seq):
    nq = seq // TQ
    return pl.pallas_call(
        _diff_kernel,
        out_shape=jax.ShapeDtypeStruct((bsz * seq, DIFF_WIDTH), BF16),
        grid=(bsz, DIFF_HEADS, nq),
        in_specs=[
            pl.BlockSpec((TQ, 2 * LANES), lambda b, h, i: (b * nq + i, COL_DQ // (2 * LANES) + h)),
            pl.BlockSpec((seq, LANES), lambda b, h, i: (b, COL_DK // LANES + h)),
            pl.BlockSpec((seq, LANES), lambda b, h, i: (b, COL_DV // LANES + h)),
            pl.BlockSpec((1, bias_t.shape[1], TQ, TQ), lambda b, h, i: (NSA_HEADS + h, 0, 0, 0)),
            pl.BlockSpec((8, LANES), lambda b, h, i: (0, 0)),
            pl.BlockSpec((1, LANES), lambda b, h, i: (0, 0)),
        ],
        out_specs=pl.BlockSpec((TQ, LANES), lambda b, h, i: (b * nq + i, h)),
        scratch_shapes=[pltpu.VMEM((2 * TQ, LANES), F32),
                        pltpu.VMEM((2 * TQ, LANES), F32),
                        pltpu.VMEM((2 * TQ, LANES), F32)],
        compiler_params=_params(("arbitrary", "arbitrary", "arbitrary")),
        name="diff_attention",
    )(main, main, main, bias_t, lam_rows, subln_g)


def _post_kernel(on_ref, od_ref, x_ref, mod_ref, g_ref, wo_ref, wq_ref, sk_ref, x1_ref, h2t_ref, subt_ref):
    mod = mod_ref[0]
    mixed = _dot(on_ref[...], wo_ref[0:NSA_WIDTH, :]) + _dot(od_ref[...], wo_ref[NSA_WIDTH:, :])
    x1 = x_ref[...] + mod[2:3] * mixed
    x1_ref[...] = x1
    h2 = _rms_mod(x1, g_ref[...], mod[4:5], mod[3:4])
    h2t_ref[...] = h2.T.astype(BF16)
    qp = _dot(h2.astype(BF16), wq_ref[...])
    for hp in range(2 * PEER_HEADS):
        qs = qp[:, hp * LANES:(hp + 1) * LANES]
        subt_ref[hp] = _dot_nt(sk_ref[hp % 2], qs, precision=lax.Precision.HIGHEST)


def _post(o_nsa, o_d, x2d, mod, g2, w_out, wq, sub_keys, seq):
    t, d = x2d.shape
    tm = 256
    per_b = seq // tm
    nq = wq.shape[1]
    return pl.pallas_call(
        _post_kernel,
        out_shape=(jax.ShapeDtypeStruct((t, d), F32),
                   jax.ShapeDtypeStruct((d, t), BF16),
                   jax.ShapeDtypeStruct((2 * PEER_HEADS, PEER_NKEYS, t), F32)),
        grid=(t // tm,),
        in_specs=[pl.BlockSpec((tm, NSA_WIDTH), lambda i: (i, 0)),
                  pl.BlockSpec((tm, DIFF_WIDTH), lambda i: (i, 0)),
                  pl.BlockSpec((tm, d), lambda i: (i, 0)),
                  pl.BlockSpec((1, 6, d), lambda i: (i // per_b, 0, 0)),
                  pl.BlockSpec((1, d), lambda i: (0, 0)),
                  pl.BlockSpec((d, d), lambda i: (0, 0)),
                  pl.BlockSpec((d, nq), lambda i: (0, 0)),
                  pl.BlockSpec((2, PEER_NKEYS, LANES), lambda i: (0, 0, 0))],
        out_specs=(pl.BlockSpec((tm, d), lambda i: (i, 0)),
                   pl.BlockSpec((d, tm), lambda i: (0, i)),
                   pl.BlockSpec((2 * PEER_HEADS, PEER_NKEYS, tm), lambda i: (0, 0, i))),
        compiler_params=_params(("arbitrary",)),
        name="post_attention",
    )(o_nsa, o_d, x2d, mod, g2, w_out, wq, sub_keys)


_PEER_PAIRS = [(r, c) for r in range(PEER_TOPK + 1) for c in range(PEER_TOPK + 1)
               if (r + 1) * (c + 1) <= PEER_TOPK + 1]


def _top_rows(x, n):
    out = []
    cur = x
    for _ in range(n):
        m = jnp.max(cur, axis=0, keepdims=True)
        out.append(m)
        cur = jnp.where(cur == m, -jnp.inf, cur)
    return out


LOG2E = 1.4426950408889634


def _peer_select_kernel(sub_ref, cnt_ref, ea_ref, rank_ref, eb_ref, cand_ref):
    cand_ref[...] = jnp.full(cand_ref.shape, -jnp.inf, F32)
    for h in range(PEER_HEADS):
        a = sub_ref[2 * h]
        b = sub_ref[2 * h + 1]
        ta = _top_rows(a, PEER_TOPK + 1)
        tb = _top_rows(b, PEER_TOPK + 1)
        for n, (r, c) in enumerate(_PEER_PAIRS):
            cand_ref[n:n + 1, :] = ta[r] + tb[c]
        cand = cand_ref[...]
        top = _top_rows(cand, PEER_TOPK + 1)
        mid = 0.5 * (top[PEER_TOPK - 1] + top[PEER_TOPK])
        best = ta[0] + tb[0]
        z = jnp.sum(jnp.where(cand > mid, jnp.exp(cand - best), 0.0), axis=0, keepdims=True)
        rank = jnp.full(b.shape, float(PEER_TOPK + 1), F32)
        cnt = jnp.zeros(a.shape, F32)
        for c in range(PEER_TOPK + 1):
            rank = jnp.where(b == tb[c], float(c), rank)
            cnt = cnt + jnp.where(a + tb[c] > mid, 1.0, 0.0)
        rank_ref[h] = rank.astype(BF16)
        cnt_ref[h] = cnt
        eb_ref[h] = jnp.exp(b - tb[0]).astype(BF16)
        ea_ref[h] = (0.5 * jnp.exp(a - ta[0])) / z


def _peer_select(subt):
    nhp, nk, t = subt.shape
    tt = 256
    wide = jax.ShapeDtypeStruct((PEER_HEADS, nk, t), F32)
    narrow = jax.ShapeDtypeStruct((PEER_HEADS, nk, t), BF16)
    spec = pl.BlockSpec((PEER_HEADS, nk, tt), lambda i: (0, 0, i))
    return pl.pallas_call(
        _peer_select_kernel,
        out_shape=(wide, wide, narrow, narrow),
        grid=(t // tt,),
        in_specs=[pl.BlockSpec((nhp, nk, tt), lambda i: (0, 0, i))],
        out_specs=(spec, spec, spec, spec),
        scratch_shapes=[pltpu.VMEM((-(-len(_PEER_PAIRS) // 8) * 8, tt), F32)],
        compiler_params=_params(("arbitrary",)),
        name="peer_select",
    )(subt)


def _peer_dense_kernel(h2t_ref, u_ref, vt_ref, cnt_ref, ea_ref, rank_ref, eb_ref, x1_ref, mod_ref, g_ref,
                       o_ref, acc_ref, y_ref, pre_ref, *, rows_per_step):
    j = pl.program_id(1)

    @pl.when(j == 0)
    def _():
        acc_ref[...] = jnp.zeros(acc_ref.shape, F32)

    tt = h2t_ref.shape[1]
    zero = jnp.zeros((PEER_NKEYS, LANES), BF16)
    c1 = math.sqrt(2.0 / math.pi)
    c2 = c1 * 0.044715
    pre_ref[...] = _dot(u_ref[...], h2t_ref[...])
    for ii in range(rows_per_step):
        r = slice(ii * PEER_NKEYS, (ii + 1) * PEER_NKEYS)
        for lt in range(tt // LANES):
            c = slice(lt * LANES, (lt + 1) * LANES)
            w = None
            for h in range(PEER_HEADS):
                keep = rank_ref[h, :, c] < cnt_ref[h, ii:ii + 1, c].astype(BF16)
                term = jnp.where(keep, eb_ref[h, :, c], zero) * ea_ref[h, ii:ii + 1, c].astype(BF16)
                w = term if w is None else w + term
            x = pre_ref[r, c]
            y_ref[r, c] = w * (x * (1.0 + jnp.tanh(x * (c1 + c2 * (x * x))))).astype(BF16)
    acc_ref[...] += _dot(vt_ref[...], y_ref[...])

    @pl.when(j == pl.num_programs(1) - 1)
    def _():
        mod = mod_ref[0]
        x2 = x1_ref[...] + mod[5:6] * acc_ref[...].T
        ms = jnp.mean(x2 * x2, axis=-1, keepdims=True)
        o_ref[...] = x2 * lax.rsqrt(ms + RMS_EPS) * g_ref[...]


def _peer_dense(h2t, u_b, vt_b, cnt, ea, rank, eb, x1, mod, final_g, seq):
    d, t = h2t.shape
    ne = u_b.shape[0]
    tt = 512
    ec = 2048
    rows = ec // PEER_NKEYS
    per_b = seq // tt
    nk = rank.shape[1]
    kernel = partial(_peer_dense_kernel, rows_per_step=rows)
    return pl.pallas_call(
        kernel,
        out_shape=jax.ShapeDtypeStruct((t, d), F32),
        grid=(t // tt, ne // ec),
        in_specs=[pl.BlockSpec((d, tt), lambda i, j: (0, i)),
                  pl.BlockSpec((ec, d), lambda i, j: (j, 0)),
                  pl.BlockSpec((d, ec), lambda i, j: (0, j)),
                  pl.BlockSpec((PEER_HEADS, rows, tt), lambda i, j: (0, j, i)),
                  pl.BlockSpec((PEER_HEADS, rows, tt), lambda i, j: (0, j, i)),
                  pl.BlockSpec((PEER_HEADS, nk, tt), lambda i, j: (0, 0, i)),
                  pl.BlockSpec((PEER_HEADS, nk, tt), lambda i, j: (0, 0, i)),
                  pl.BlockSpec((tt, d), lambda i, j: (i, 0)),
                  pl.BlockSpec((1, 6, d), lambda i, j: (i // per_b, 0, 0)),
                  pl.BlockSpec((1, d), lambda i, j: (0, 0))],
        out_specs=pl.BlockSpec((tt, d), lambda i, j: (i, 0)),
        scratch_shapes=[pltpu.VMEM((d, tt), F32), pltpu.VMEM((ec, tt), BF16), pltpu.VMEM((ec, tt), F32)],
        compiler_params=_params(("arbitrary", "arbitrary")),
        name="peer_dense",
    )(h2t, u_b, vt_b, cnt, ea, rank, eb, x1, mod, final_g)


def _t5_bucket_np(dist):
    n = np.maximum(dist, 0)
    max_exact = REL_BUCKETS // 2
    nf = np.maximum(n, 1).astype(np.float32)
    large = max_exact + (np.log(nf / np.float32(max_exact)) / np.float32(math.log(REL_MAX_DIST / max_exact))
                         * np.float32(REL_BUCKETS - max_exact)).astype(np.int32)
    return np.where(n < max_exact, n, np.minimum(large, REL_BUCKETS - 1)).astype(np.int32)


def _token_bucket_tiles():
    i = np.arange(TQ)[:, None]
    j = np.arange(TQ)[None, :]
    d0 = i - j
    d1 = TQ + i - j
    d2 = WINDOW + i - j
    assert WINDOW == 2 * TQ and _t5_bucket_np(np.array([TQ + 1]))[0] == REL_BUCKETS - 1
    bkt = _t5_bucket_np(np.stack([d0, d1, d2]))
    valid = np.stack([d0 >= 0, np.ones_like(d0, bool), d2 < WINDOW])
    return np.where(valid, bkt, -1).astype(np.int32)


def _cmp_bucket_tile():
    i = np.arange(TQ)[:, None]
    c = np.arange(CMP_NEAR)[None, :] - CMP_NEAR // 2
    d = i - (c * CMP_STRIDE + CMP_BLOCK - 1)
    assert _t5_bucket_np(np.array([(CMP_NEAR // 2 + 1) * CMP_STRIDE - CMP_BLOCK + 1]))[0] == REL_BUCKETS - 1
    assert (CMP_NEAR // 2) * CMP_STRIDE + CMP_BLOCK - 1 > TQ - 1
    return np.where(d >= 0, _t5_bucket_np(d), REL_BUCKETS - 1).astype(np.int32)


def _block_overlap_t(nb, n_sel):
    cmp_start = np.arange(nb) * CMP_STRIDE
    sel_start = np.arange(n_sel) * SEL_BLOCK
    lo = np.maximum(cmp_start[None, :], sel_start[:, None])
    hi = np.minimum(cmp_start[None, :] + CMP_BLOCK, sel_start[:, None] + SEL_BLOCK)
    ov = np.maximum(hi - lo, 0).astype(np.float32) / CMP_BLOCK
    ov[:, nb - 1] = 0.0
    return ov


def _arrange_w_in(w):
    d = w.shape[0]
    scale = HEAD_DIM ** -0.5
    o_kv = NSA_WIDTH
    o_gate = o_kv + 3 * 2 * NSA_GROUPS * HEAD_DIM
    o_dq = o_gate + NSA_HEADS * 3
    o_dk = o_dq + DIFF_WIDTH
    o_dv = o_dk + DIFF_WIDTH
    z64 = lambda *lead: jnp.zeros((d,) + lead + (HEAD_DIM,), w.dtype)
    qn = (w[:, :o_kv] * scale).reshape(d, NSA_HEADS, HEAD_DIM)
    qn = jnp.concatenate([qn, z64(NSA_HEADS)], axis=-1).reshape(d, NSA_HEADS * LANES)
    kv = w[:, o_kv:o_gate].reshape(d, 3, 2, NSA_GROUPS, HEAD_DIM)
    gates = w[:, o_gate:o_dq].reshape(d, NSA_GROUPS, NSA_HPG * 3)
    dq = (w[:, o_dq:o_dk] * scale).reshape(d, DIFF_HEADS, 2, HEAD_DIM)
    dq = jnp.concatenate([dq[:, :, 0], z64(DIFF_HEADS), z64(DIFF_HEADS), dq[:, :, 1]], axis=-1)
    dq = dq.reshape(d, DIFF_HEADS * 2 * LANES)
    dk = w[:, o_dk:o_dv]
    dv = w[:, o_dv:]
    kvc = kv[:, 0].reshape(d, 2 * NSA_GROUPS * HEAD_DIM)
    kpad = lambda k: jnp.concatenate([k, z64(NSA_GROUPS)], axis=-1).reshape(d, NSA_GROUPS * LANES)
    vdup = lambda v: jnp.concatenate([v, v], axis=-1).reshape(d, NSA_GROUPS * LANES)
    gpad = jnp.pad(gates, ((0, 0), (0, 0), (0, LANES - NSA_HPG * 3))).reshape(d, N_GATE)
    cols = [qn, dq, dk, dv, kvc, kpad(kv[:, 1, 0]), kpad(kv[:, 2, 0]), vdup(kv[:, 1, 1]), vdup(kv[:, 2, 1]), gpad]
    return jnp.concatenate(cols, axis=1).astype(BF16)


def kernel(x, c, rel_bias, ada_w, ada_b, norm1_g, norm2_g, w_in, w_out, cmp_pos, cmp_w1, cmp_w2, lam_q1, lam_k1,
           lam_q2, lam_k2, diff_subln_g, peer_wq, peer_sub_keys, peer_u, peer_v, final_g):
    bsz, seq, d = x.shape
    t = bsz * seq
    nb = seq // CMP_STRIDE
    n_sel = seq // SEL_BLOCK
    assert seq % TK_FAR == 0 and n_sel <= HEAD_DIM
    x2d = x.reshape(t, d)

    mod = _modulation(c, ada_w[0], ada_b[0]).reshape(bsz, 6, d)
    main, gates = _in_proj(x2d, mod, norm1_g[0].reshape(1, d), _arrange_w_in(w_in[0]), seq)

    tokr = (main[:, COL_KVC:COL_KVC + 4 * HEAD_DIM].reshape(bsz, nb, CMP_STRIDE, 4, HEAD_DIM)
            .transpose(0, 3, 1, 2, 4).reshape(bsz, 4, nb, CMP_STRIDE * HEAD_DIM))
    w2dup = jnp.concatenate([cmp_w2[0], cmp_w2[0]], axis=-1).astype(BF16)
    kvcmp = _compress(tokr, cmp_w1[0].astype(BF16), cmp_pos[0].reshape(2, 1, CMP_BLOCK * HEAD_DIM), w2dup)

    bias_t, bias_c = _bias_tables(rel_bias.astype(F32).T.reshape(-1), jnp.asarray(_token_bucket_tiles()),
                                  jnp.asarray(_cmp_bucket_tile()))
    cts_t = jnp.asarray(_block_overlap_t(nb, n_sel))
    place = jnp.asarray(np.eye(n_sel, LANES, HEAD_DIM, dtype=np.float32), BF16)

    o_nsa = _nsa(main, kvcmp, gates, bias_c, bias_t, cts_t, place, bsz, seq)

    lam_rows = jnp.pad(jnp.stack([lam_q1[0], lam_k1[0], lam_q2[0], lam_k2[0]]).astype(F32),
                       ((0, 4), (0, LANES - lam_q1.shape[1])))
    o_d = _diff(main, bias_t, lam_rows, diff_subln_g[0].reshape(1, DIFF_V_DIM), bsz, seq)

    x1, h2t, subt = _post(o_nsa, o_d, x2d, mod, norm2_g[0].reshape(1, d), w_out[0].astype(BF16),
                          peer_wq[0].astype(BF16), peer_sub_keys[0], seq)
    cnt, ea, rank, eb = _peer_select(subt)
    out = _peer_dense(h2t, peer_u[0].astype(BF16), peer_v[0].T.astype(BF16), cnt, ea, rank, eb, x1, mod,
                      final_g.reshape(1, d), seq)
    return out.reshape(bsz, seq, d)
```

```python
import math
from functools import partial

import numpy as np
import jax
import jax.numpy as jnp
from jax import lax
from jax.experimental import pallas as pl
from jax.experimental.pallas import tpu as pltpu

F32 = jnp.float32
BF16 = jnp.bfloat16

D_MODEL = 1024
HEAD_DIM = 64
NSA_GROUPS = 2
NSA_HPG = 4
NSA_HEADS = NSA_GROUPS * NSA_HPG
NSA_WIDTH = NSA_HEADS * HEAD_DIM
CMP_BLOCK = 32
CMP_STRIDE = 16
CMP_HIDDEN = 256
SEL_BLOCK = 64
SEL_TOPN = 16
WINDOW = 512
DIFF_HEADS = 4
DIFF_V_DIM = 128
DIFF_WIDTH = DIFF_HEADS * DIFF_V_DIM
N_ATTN_HEADS = NSA_HEADS + DIFF_HEADS
REL_BUCKETS = 32
REL_MAX_DIST = 128
PEER_HEADS = 8
PEER_NKEYS = 128
PEER_EXPERTS = PEER_NKEYS ** 2
PEER_TOPK = 16
RMS_EPS = 1e-6
NEG = -1e30
FORCED_SCORE = 1e9
LAM_INIT = 0.8 - 0.6 * math.exp(-0.3 * 0)

LANES = 128
TQ = 256
TK_FAR = 512
CMP_NEAR = 32
VMEM_LIMIT = 56 * 1024 * 1024

COL_QN = 0
COL_DQ = 1024
COL_DK = 2048
COL_DV = 2560
COL_KVC = 3072
COL_KSEL = 3328
COL_KWIN = 3584
COL_VSEL = 3840
COL_VWIN = 4096
N_MAIN = 4352
N_GATE = 256
N_ALL = N_MAIN + N_GATE


def _dot(a, b, precision=None):
    return jnp.dot(a, b, preferred_element_type=F32, precision=precision)


def _dot_nt(a, b, precision=None):
    return lax.dot_general(a, b, (((1,), (1,)), ((), ())), preferred_element_type=F32, precision=precision)


def _params(sem):
    return pltpu.CompilerParams(dimension_semantics=sem, vmem_limit_bytes=VMEM_LIMIT)


def _mod_kernel(c_ref, w_ref, b_ref, o_ref):
    c = c_ref[...]
    cond = c * jax.nn.sigmoid(c)
    o_ref[...] = _dot(cond, w_ref[...], precision=lax.Precision.HIGHEST) + b_ref[...]


def _modulation(c, ada_w, ada_b):
    bsz, d = c.shape
    n = ada_w.shape[1]
    tn = 512
    return pl.pallas_call(
        _mod_kernel,
        out_shape=jax.ShapeDtypeStruct((bsz, n), F32),
        grid=(n // tn,),
        in_specs=[pl.BlockSpec((bsz, d), lambda j: (0, 0)),
                  pl.BlockSpec((d, tn), lambda j: (0, j)),
                  pl.BlockSpec((1, tn), lambda j: (0, j))],
        out_specs=pl.BlockSpec((bsz, tn), lambda j: (0, j)),
        compiler_params=_params(("arbitrary",)),
        name="modulation",
    )(c, ada_w, ada_b.reshape(1, n))


def _rms_mod(x, g, scale, shift):
    ms = jnp.mean(x * x, axis=-1, keepdims=True)
    y = x * lax.rsqrt(ms + RMS_EPS) * g
    return y * (1.0 + scale) + shift


def _inproj_kernel(x_ref, mod_ref, g_ref, w_ref, main_ref, gate_ref, *, tiles_per_seq):
    mod = mod_ref[0]
    h = _rms_mod(x_ref[...], g_ref[...], mod[1:2], mod[0:1]).astype(BF16)
    tm = h.shape[0]
    step = 512
    for c0 in range(0, N_MAIN, step):
        c1 = min(c0 + step, N_MAIN)
        main_ref[:, c0:c1] = _dot(h, w_ref[:, c0:c1]).astype(BF16)
    gate_ref[...] = _dot(h, w_ref[:, N_MAIN:N_ALL])
    pos = (pl.program_id(0) % tiles_per_seq) * tm + lax.broadcasted_iota(jnp.int32, (tm, LANES), 0)
    lane = lax.broadcasted_iota(jnp.int32, (tm, LANES), 1)
    onehot = jnp.where(lane == HEAD_DIM + pos // SEL_BLOCK, 1.0, 0.0).astype(BF16)
    for g in range(NSA_GROUPS):
        c0 = COL_KSEL + g * LANES
        main_ref[:, c0:c0 + LANES] = main_ref[:, c0:c0 + LANES] + onehot


def _in_proj(x2d, mod, g1, w_all, seq):
    t, d = x2d.shape
    tm = 512
    per_b = seq // tm
    return pl.pallas_call(
        partial(_inproj_kernel, tiles_per_seq=per_b),
        out_shape=(jax.ShapeDtypeStruct((t, N_MAIN), BF16), jax.ShapeDtypeStruct((t, N_GATE), F32)),
        grid=(t // tm,),
        in_specs=[pl.BlockSpec((tm, d), lambda i: (i, 0)),
                  pl.BlockSpec((1, 6, d), lambda i: (i // per_b, 0, 0)),
                  pl.BlockSpec((1, d), lambda i: (0, 0)),
                  pl.BlockSpec((d, N_ALL), lambda i: (0, 0))],
        out_specs=(pl.BlockSpec((tm, N_MAIN), lambda i: (i, 0)),
                   pl.BlockSpec((tm, N_GATE), lambda i: (i, 0))),
        compiler_params=_params(("arbitrary",)),
        name="in_proj",
    )(x2d, mod, g1, w_all)


def _compress_kernel(tok_ref, w1_ref, pos_ref, w2_ref, o_ref):
    tok = tok_ref[0, 0]
    half = CMP_STRIDE * HEAD_DIM
    first = _dot(tok, w1_ref[0, :half, :])
    second = _dot(tok, w1_ref[0, half:, :])
    posb = _dot(pos_ref[0], w1_ref[0].astype(F32), precision=lax.Precision.HIGHEST)
    hidden = first + jnp.concatenate([second[1:], second[:1]], axis=0) + posb
    o_ref[0, 0] = _dot(jax.nn.gelu(hidden).astype(BF16), w2_ref[0]).astype(BF16)


def _compress(tokr, w1, posflat, w2dup):
    bsz, _, nb, kdim = tokr.shape
    return pl.pallas_call(
        _compress_kernel,
        out_shape=jax.ShapeDtypeStruct((bsz, 4, nb, LANES), BF16),
        grid=(bsz, 4),
        in_specs=[pl.BlockSpec((1, 1, nb, kdim), lambda b, j: (b, j, 0, 0)),
                  pl.BlockSpec((1, 2 * kdim, CMP_HIDDEN), lambda b, j: (j // 2, 0, 0)),
                  pl.BlockSpec((1, 1, 2 * kdim), lambda b, j: (j // 2, 0, 0)),
                  pl.BlockSpec((1, CMP_HIDDEN, LANES), lambda b, j: (j // 2, 0, 0))],
        out_specs=pl.BlockSpec((1, 1, nb, LANES), lambda b, j: (b, j, 0, 0)),
        compiler_params=_params(("arbitrary", "arbitrary")),
        name="compress",
    )(tokr, w1, posflat, w2dup)


def _bias_kernel(tab_ref, bt_ref, bc_ref, ot_ref, oc_ref):
    h = pl.program_id(0)
    far = tab_ref[h * REL_BUCKETS + REL_BUCKETS - 1]

    def lookup(bkt, masked):
        acc = jnp.full(bkt.shape, masked, F32)
        for b in range(REL_BUCKETS):
            acc = jnp.where(bkt == b, tab_ref[h * REL_BUCKETS + b] - far, acc)
        return acc

    ot_ref[0] = lookup(bt_ref[...], NEG)
    oc_ref[0] = lookup(bc_ref[...], 0.0)


def _bias_tables(tab_flat, bkt_t, bkt_c):
    nt = bkt_t.shape[0]
    return pl.pallas_call(
        _bias_kernel,
        out_shape=(jax.ShapeDtypeStruct((N_ATTN_HEADS, nt, TQ, TQ), F32),
                   jax.ShapeDtypeStruct((N_ATTN_HEADS, TQ, CMP_NEAR), F32)),
        grid_spec=pltpu.PrefetchScalarGridSpec(
            num_scalar_prefetch=1,
            grid=(N_ATTN_HEADS,),
            in_specs=[pl.BlockSpec((nt, TQ, TQ), lambda h, tab: (0, 0, 0)),
                      pl.BlockSpec((TQ, CMP_NEAR), lambda h, tab: (0, 0))],
            out_specs=(pl.BlockSpec((1, nt, TQ, TQ), lambda h, tab: (h, 0, 0, 0)),
                       pl.BlockSpec((1, TQ, CMP_NEAR), lambda h, tab: (h, 0, 0))),
        ),
        compiler_params=_params(("arbitrary",)),
        name="bias_tables",
    )(tab_flat, bkt_t, bkt_c)


def _flash_reset(m_ref, l_ref, acc_ref):
    m_ref[...] = jnp.full(m_ref.shape, 3.0 * NEG, F32)
    l_ref[...] = jnp.zeros(l_ref.shape, F32)
    acc_ref[...] = jnp.zeros(acc_ref.shape, F32)


def _flash_step(q_all, k, v, bias, m_ref, l_ref, acc_ref):
    s = _dot_nt(q_all, k)
    if bias is not None:
        s = s + bias
    m_prev = m_ref[...]
    m_next = jnp.maximum(m_prev, jnp.max(s, axis=1, keepdims=True))
    alpha = jnp.exp(m_prev - m_next)
    p = jnp.exp(s - jnp.tile(m_next, (1, s.shape[1] // LANES)))
    l_ref[...] = alpha * l_ref[...] + jnp.sum(p, axis=1, keepdims=True)
    acc_ref[...] = alpha * acc_ref[...] + _dot(p.astype(BF16), v)
    m_ref[...] = m_next


def _rows(ref, start, size):
    return ref[pl.ds(pl.multiple_of(start, TQ), size), :]


def _causal_flash(q_all, k_ref, v_ref, qi, bias_near, m_ref, l_ref, acc_ref):
    _flash_reset(m_ref, l_ref, acc_ref)
    far_tiles = jnp.maximum(qi - 1, 0)
    per = TK_FAR // TQ

    def far_body(j, carry):
        _flash_step(q_all, _rows(k_ref, j * TK_FAR, TK_FAR), _rows(v_ref, j * TK_FAR, TK_FAR), None,
                    m_ref, l_ref, acc_ref)
        return carry

    lax.fori_loop(0, far_tiles // per, far_body, 0)

    @pl.when(far_tiles % per == 1)
    def _():
        start = (far_tiles - 1) * TQ
        _flash_step(q_all, _rows(k_ref, start, TQ), _rows(v_ref, start, TQ), None, m_ref, l_ref, acc_ref)

    @pl.when(qi >= 1)
    def _():
        _flash_step(q_all, _rows(k_ref, (qi - 1) * TQ, TQ), _rows(v_ref, (qi - 1) * TQ, TQ), bias_near(1),
                    m_ref, l_ref, acc_ref)

    _flash_step(q_all, _rows(k_ref, qi * TQ, TQ), _rows(v_ref, qi * TQ, TQ), bias_near(0), m_ref, l_ref, acc_ref)
    return acc_ref[...] / l_ref[...]


def _nsa_kernel(q_ref, kc_ref, vc_ref, ks_ref, vs_ref, kw_ref, vw_ref, gate_ref, bc_ref, bt_ref, cts_ref, place_ref,
                o_ref, m_ref, l_ref, acc_ref, *, n_top):
    qi = pl.program_id(2)
    rows = NSA_HPG * TQ
    q = q_ref[...]
    q_all = jnp.concatenate([q[:, h * LANES:(h + 1) * LANES] for h in range(NSA_HPG)], axis=0)

    kc = kc_ref[0, 0]
    nb = kc.shape[0]
    c0 = qi * (TQ // CMP_STRIDE)
    pr = lax.broadcasted_iota(jnp.int32, (CMP_NEAR, nb), 0)
    pc = lax.broadcasted_iota(jnp.int32, (CMP_NEAR, nb), 1)
    placement = jnp.where(pr + (c0 - CMP_NEAR // 2) == pc, 1.0, 0.0)
    bias_c = _dot(bc_ref[...].reshape(rows, CMP_NEAR), placement, precision=lax.Precision.HIGHEST)
    blk_c = lax.broadcasted_iota(jnp.int32, (rows, nb), 1)
    t_c = qi * TQ + lax.broadcasted_iota(jnp.int32, (rows, nb), 0) % TQ
    valid = (blk_c * CMP_STRIDE + (CMP_BLOCK - 1) <= t_c) & (blk_c < nb - 1)
    s_c = jnp.where(valid, _dot_nt(q_all, kc) + bias_c, NEG)
    m_c = jnp.max(s_c, axis=1, keepdims=True)
    p_c = jnp.where(valid, jnp.exp(s_c - m_c), 0.0)
    l_c = jnp.sum(p_c, axis=1, keepdims=True)
    p_c = p_c / jnp.where(l_c > 0.0, l_c, 1.0)
    o_c = _dot(p_c.astype(BF16), vc_ref[0, 0])

    p_sum = p_c[0:TQ] + p_c[TQ:2 * TQ] + p_c[2 * TQ:3 * TQ] + p_c[3 * TQ:4 * TQ]
    imp_t = _dot_nt(cts_ref[...], p_sum, precision=lax.Precision.HIGHEST)
    ns = imp_t.shape[0]
    blk_id = lax.broadcasted_iota(jnp.int32, (ns, TQ), 0)
    tpos = qi * TQ + lax.broadcasted_iota(jnp.int32, (ns, TQ), 1)
    cur = tpos // SEL_BLOCK
    forced = (blk_id == 0) | (blk_id == cur) | (blk_id == cur - 1)
    score = jnp.where(forced, FORCED_SCORE, jnp.where(blk_id <= cur, imp_t, NEG))
    rank = jnp.zeros((ns, TQ), F32)
    for k in range(ns):
        rk = score[k:k + 1, :]
        beats = (rk > score) | ((rk == score) & (blk_id > k))
        rank = rank + jnp.where(beats, 1.0, 0.0)
    sel = jnp.where(rank < float(n_top), 1.0, 0.0).T.astype(BF16)
    placed = _dot(sel, place_ref[...])
    lane = lax.broadcasted_iota(jnp.int32, (TQ, LANES), 1)
    mask_lanes = jnp.where(lane >= HEAD_DIM, (placed - 1.0) * (-NEG), 0.0).astype(BF16)
    q_sel = q_all + jnp.concatenate([mask_lanes] * NSA_HPG, axis=0)

    def bias_tile(idx):
        return bt_ref[:, idx].reshape(rows, TQ)

    o_s = _causal_flash(q_sel, ks_ref, vs_ref, qi, bias_tile, m_ref, l_ref, acc_ref)

    _flash_reset(m_ref, l_ref, acc_ref)

    @pl.when(qi >= 2)
    def _():
        _flash_step(q_all, _rows(kw_ref, (qi - 2) * TQ, TQ), _rows(vw_ref, (qi - 2) * TQ, TQ), bias_tile(2),
                    m_ref, l_ref, acc_ref)

    @pl.when(qi >= 1)
    def _():
        _flash_step(q_all, _rows(kw_ref, (qi - 1) * TQ, TQ), _rows(vw_ref, (qi - 1) * TQ, TQ), bias_tile(1),
                    m_ref, l_ref, acc_ref)

    _flash_step(q_all, _rows(kw_ref, qi * TQ, TQ), _rows(vw_ref, qi * TQ, TQ), bias_tile(0), m_ref, l_ref, acc_ref)
    o_w = acc_ref[...] / l_ref[...]

    gt = jax.nn.sigmoid(gate_ref[...])
    low = lane < HEAD_DIM
    outs = []
    for h in range(NSA_HPG):
        r = slice(h * TQ, (h + 1) * TQ)
        outs.append(gt[:, 3 * h:3 * h + 1] * o_c[r] + gt[:, 3 * h + 1:3 * h + 2] * o_s[r]
                    + gt[:, 3 * h + 2:3 * h + 3] * o_w[r])
    o_ref[:, 0:LANES] = jnp.where(low, outs[0], outs[1]).astype(BF16)
    o_ref[:, LANES:2 * LANES] = jnp.where(low, outs[2], outs[3]).astype(BF16)


def _nsa(main, kvcmp, gates, bias_c, bias_t, cts_t, place, bsz, seq):
    nq = seq // TQ
    nb = kvcmp.shape[2]
    ns = cts_t.shape[0]
    nt = bias_t.shape[1]
    kernel = partial(_nsa_kernel, n_top=min(SEL_TOPN, ns))
    col = lambda c: c // LANES
    return pl.pallas_call(
        kernel,
        out_shape=jax.ShapeDtypeStruct((bsz * seq, NSA_WIDTH), BF16),
        grid=(bsz, NSA_GROUPS, nq),
        in_specs=[
            pl.BlockSpec((TQ, NSA_HPG * LANES), lambda b, g, i: (b * nq + i, g)),
            pl.BlockSpec((1, 1, nb, LANES), lambda b, g, i: (b, g, 0, 0)),
            pl.BlockSpec((1, 1, nb, LANES), lambda b, g, i: (b, 2 + g, 0, 0)),
            pl.BlockSpec((seq, LANES), lambda b, g, i: (b, col(COL_KSEL) + g)),
            pl.BlockSpec((seq, LANES), lambda b, g, i: (b, col(COL_VSEL) + g)),
            pl.BlockSpec((seq, LANES), lambda b, g, i: (b, col(COL_KWIN) + g)),
            pl.BlockSpec((seq, LANES), lambda b, g, i: (b, col(COL_VWIN) + g)),
            pl.BlockSpec((TQ, LANES), lambda b, g, i: (b * nq + i, g)),
            pl.BlockSpec((NSA_HPG, TQ, CMP_NEAR), lambda b, g, i: (g, 0, 0)),
            pl.BlockSpec((NSA_HPG, nt, TQ, TQ), lambda b, g, i: (g, 0, 0, 0)),
            pl.BlockSpec((ns, nb), lambda b, g, i: (0, 0)),
            pl.BlockSpec((ns, LANES), lambda b, g, i: (0, 0)),
        ],
        out_specs=pl.BlockSpec((TQ, 2 * LANES), lambda b, g, i: (b * nq + i, g)),
        scratch_shapes=[pltpu.VMEM((NSA_HPG * TQ, LANES), F32),
                        pltpu.VMEM((NSA_HPG * TQ, LANES), F32),
                        pltpu.VMEM((NSA_HPG * TQ, LANES), F32)],
        compiler_params=_params(("arbitrary", "arbitrary", "arbitrary")),
        name="nsa_attention",
    )(main, kvcmp, kvcmp, main, main, main, main, gates, bias_c, bias_t, cts_t, place)


def _diff_kernel(q_ref, k_ref, v_ref, bt_ref, lam_ref, g_ref, o_ref, m_ref, l_ref, acc_ref):
    qi = pl.program_id(2)
    q = q_ref[...]
    q_all = jnp.concatenate([q[:, 0:LANES], q[:, LANES:2 * LANES]], axis=0)

    def bias_tile(idx):
        b = bt_ref[0, idx]
        return jnp.concatenate([b, b], axis=0)

    o = _causal_flash(q_all, k_ref, v_ref, qi, bias_tile, m_ref, l_ref, acc_ref)

    lam_rows = lam_ref[...]
    lam = (jnp.exp(jnp.sum(lam_rows[0:1] * lam_rows[1:2], axis=1, keepdims=True))
           - jnp.exp(jnp.sum(lam_rows[2:3] * lam_rows[3:4], axis=1, keepdims=True)) + LAM_INIT)
    od = o[0:TQ] - lam * o[TQ:2 * TQ]
    ms = jnp.mean(od * od, axis=-1, keepdims=True)
    o_ref[...] = (od * lax.rsqrt(ms + RMS_EPS) * g_ref[...] * (1.0 - LAM_INIT)).astype(BF16)


def _diff(main, bias_t, lam_rows, subln_g, bsz, seq):
    nq = seq // TQ
    return pl.pallas_call(
        _diff_kernel,
        out_shape=jax.ShapeDtypeStruct((bsz * seq, DIFF_WIDTH), BF16),
        grid=(bsz, DIFF_HEADS, nq),
        in_specs=[
            pl.BlockSpec((TQ, 2 * LANES), lambda b, h, i: (b * nq + i, COL_DQ // (2 * LANES) + h)),
            pl.BlockSpec((seq, LANES), lambda b, h, i: (b, COL_DK // LANES + h)),
            pl.BlockSpec((seq, LANES), lambda b, h, i: (b, COL_DV // LANES + h)),
            pl.BlockSpec((1, bias_t.shape[1], TQ, TQ), lambda b, h, i: (NSA_HEADS + h, 0, 0, 0)),
            pl.BlockSpec((8, LANES), lambda b, h, i: (0, 0)),
            pl.BlockSpec((1, LANES), lambda b, h, i: (0, 0)),
        ],
        out_specs=pl.BlockSpec((TQ, LANES), lambda b, h, i: (b * nq + i, h)),
        scratch_shapes=[pltpu.VMEM((2 * TQ, LANES), F32),
                        pltpu.VMEM((2 * TQ, LANES), F32),
                        pltpu.VMEM((2 * TQ, LANES), F32)],
        compiler_params=_params(("arbitrary", "arbitrary", "arbitrary")),
        name="diff_attention",
    )(main, main, main, bias_t, lam_rows, subln_g)


def _post_kernel(on_ref, od_ref, x_ref, mod_ref, g_ref, wo_ref, wq_ref, sk_ref, x1_ref, h2t_ref, subt_ref):
    mod = mod_ref[0]
    mixed = _dot(on_ref[...], wo_ref[0:NSA_WIDTH, :]) + _dot(od_ref[...], wo_ref[NSA_WIDTH:, :])
    x1 = x_ref[...] + mod[2:3] * mixed
    x1_ref[...] = x1
    h2 = _rms_mod(x1, g_ref[...], mod[4:5], mod[3:4])
    h2t_ref[...] = h2.T.astype(BF16)
    qp = _dot(h2.astype(BF16), wq_ref[...])
    for hp in range(2 * PEER_HEADS):
        qs = qp[:, hp * LANES:(hp + 1) * LANES]
        subt_ref[hp] = _dot_nt(sk_ref[hp % 2], qs, precision=lax.Precision.HIGHEST)


def _post(o_nsa, o_d, x2d, mod, g2, w_out, wq, sub_keys, seq):
    t, d = x2d.shape
    tm = 256
    per_b = seq // tm
    nq = wq.shape[1]
    return pl.pallas_call(
        _post_kernel,
        out_shape=(jax.ShapeDtypeStruct((t, d), F32),
                   jax.ShapeDtypeStruct((d, t), BF16),
                   jax.ShapeDtypeStruct((2 * PEER_HEADS, PEER_NKEYS, t), F32)),
        grid=(t // tm,),
        in_specs=[pl.BlockSpec((tm, NSA_WIDTH), lambda i: (i, 0)),
                  pl.BlockSpec((tm, DIFF_WIDTH), lambda i: (i, 0)),
                  pl.BlockSpec((tm, d), lambda i: (i, 0)),
                  pl.BlockSpec((1, 6, d), lambda i: (i // per_b, 0, 0)),
                  pl.BlockSpec((1, d), lambda i: (0, 0)),
                  pl.BlockSpec((d, d), lambda i: (0, 0)),
                  pl.BlockSpec((d, nq), lambda i: (0, 0)),
                  pl.BlockSpec((2, PEER_NKEYS, LANES), lambda i: (0, 0, 0))],
        out_specs=(pl.BlockSpec((tm, d), lambda i: (i, 0)),
                   pl.BlockSpec((d, tm), lambda i: (0, i)),
                   pl.BlockSpec((2 * PEER_HEADS, PEER_NKEYS, tm), lambda i: (0, 0, i))),
        compiler_params=_params(("arbitrary",)),
        name="post_attention",
    )(o_nsa, o_d, x2d, mod, g2, w_out, wq, sub_keys)


_PEER_PAIRS = [(r, c) for r in range(PEER_TOPK + 1) for c in range(PEER_TOPK + 1)
               if (r + 1) * (c + 1) <= PEER_TOPK + 1]


def _top_rows(x, n):
    out = []
    cur = x
    for _ in range(n):
        m = jnp.max(cur, axis=0, keepdims=True)
        out.append(m)
        cur = jnp.where(cur == m, -jnp.inf, cur)
    return out


LOG2E = 1.4426950408889634


def _peer_select_kernel(sub_ref, cnt_ref, ea_ref, rank_ref, eb_ref, cand_ref):
    cand_ref[...] = jnp.full(cand_ref.shape, -jnp.inf, F32)
    for h in range(PEER_HEADS):
        a = sub_ref[2 * h]
        b = sub_ref[2 * h + 1]
        ta = _top_rows(a, PEER_TOPK + 1)
        tb = _top_rows(b, PEER_TOPK + 1)
        for n, (r, c) in enumerate(_PEER_PAIRS):
            cand_ref[n:n + 1, :] = ta[r] + tb[c]
        cand = cand_ref[...]
        top = _top_rows(cand, PEER_TOPK + 1)
        mid = 0.5 * (top[PEER_TOPK - 1] + top[PEER_TOPK])
        best = ta[0] + tb[0]
        z = jnp.sum(jnp.where(cand > mid, jnp.exp(cand - best), 0.0), axis=0, keepdims=True)
        rank = jnp.full(b.shape, float(PEER_TOPK + 1), F32)
        cnt = jnp.zeros(a.shape, F32)
        for c in range(PEER_TOPK + 1):
            rank = jnp.where(b == tb[c], float(c), rank)
            cnt = cnt + jnp.where(a + tb[c] > mid, 1.0, 0.0)
        rank_ref[h] = rank.astype(BF16)
        cnt_ref[h] = cnt
        eb_ref[h] = jnp.exp(b - tb[0]).astype(BF16)
        ea_ref[h] = (0.5 * jnp.exp(a - ta[0])) / z


def _peer_select(subt):
    nhp, nk, t = subt.shape
    tt = 256
    wide = jax.ShapeDtypeStruct((PEER_HEADS, nk, t), F32)
    narrow = jax.ShapeDtypeStruct((PEER_HEADS, nk, t), BF16)
    spec = pl.BlockSpec((PEER_HEADS, nk, tt), lambda i: (0, 0, i))
    return pl.pallas_call(
        _peer_select_kernel,
        out_shape=(wide, wide, narrow, narrow),
        grid=(t // tt,),
        in_specs=[pl.BlockSpec((nhp, nk, tt), lambda i: (0, 0, i))],
        out_specs=(spec, spec, spec, spec),
        scratch_shapes=[pltpu.VMEM((-(-len(_PEER_PAIRS) // 8) * 8, tt), F32)],
        compiler_params=_params(("arbitrary",)),
        name="peer_select",
    )(subt)


def _peer_dense_kernel(h2t_ref, u_ref, vt_ref, cnt_ref, ea_ref, rank_ref, eb_ref, x1_ref, mod_ref, g_ref,
                       o_ref, acc_ref, y_ref, pre_ref, *, rows_per_step):
    j = pl.program_id(1)

    @pl.when(j == 0)
    def _():
        acc_ref[...] = jnp.zeros(acc_ref.shape, F32)

    tt = h2t_ref.shape[1]
    zero = jnp.zeros((PEER_NKEYS, LANES), BF16)
    c1 = math.sqrt(2.0 / math.pi)
    c2 = c1 * 0.044715
    quarter = rows_per_step // 4
    for ii in range(rows_per_step):
        if ii % quarter == 0:
            q = slice(ii * PEER_NKEYS, (ii + quarter) * PEER_NKEYS)
            pre_ref[q, :] = _dot(u_ref[q, :], h2t_ref[...])
        r = slice(ii * PEER_NKEYS, (ii + 1) * PEER_NKEYS)
        for lt in range(tt // LANES):
            c = slice(lt * LANES, (lt + 1) * LANES)
            w = None
            for h in range(PEER_HEADS):
                keep = rank_ref[h, :, c] < cnt_ref[h, ii:ii + 1, c].astype(BF16)
                term = jnp.where(keep, eb_ref[h, :, c], zero) * ea_ref[h, ii:ii + 1, c].astype(BF16)
                w = term if w is None else w + term
            x = pre_ref[r, c]
            y_ref[r, c] = w * (x * (1.0 + jnp.tanh(x * (c1 + c2 * (x * x))))).astype(BF16)
    acc_ref[...] += _dot(vt_ref[...], y_ref[...])

    @pl.when(j == pl.num_programs(1) - 1)
    def _():
        mod = mod_ref[0]
        x2 = x1_ref[...] + mod[5:6] * acc_ref[...].T
        ms = jnp.mean(x2 * x2, axis=-1, keepdims=True)
        o_ref[...] = x2 * lax.rsqrt(ms + RMS_EPS) * g_ref[...]


def _peer_dense(h2t, u_b, vt_b, cnt, ea, rank, eb, x1, mod, final_g, seq):
    d, t = h2t.shape
    ne = u_b.shape[0]
    tt = 512
    ec = 2048
    rows = ec // PEER_NKEYS
    per_b = seq // tt
    nk = rank.shape[1]
    kernel = partial(_peer_dense_kernel, rows_per_step=rows)
    return pl.pallas_call(
        kernel,
        out_shape=jax.ShapeDtypeStruct((t, d), F32),
        grid=(t // tt, ne // ec),
        in_specs=[pl.BlockSpec((d, tt), lambda i, j: (0, i)),
                  pl.BlockSpec((ec, d), lambda i, j: (j, 0)),
                  pl.BlockSpec((d, ec), lambda i, j: (0, j)),
                  pl.BlockSpec((PEER_HEADS, rows, tt), lambda i, j: (0, j, i)),
                  pl.BlockSpec((PEER_HEADS, rows, tt), lambda i, j: (0, j, i)),
                  pl.BlockSpec((PEER_HEADS, nk, tt), lambda i, j: (0, 0, i)),
                  pl.BlockSpec((PEER_HEADS, nk, tt), lambda i, j: (0, 0, i)),
                  pl.BlockSpec((tt, d), lambda i, j: (i, 0)),
                  pl.BlockSpec((1, 6, d), lambda i, j: (i // per_b, 0, 0)),
                  pl.BlockSpec((1, d), lambda i, j: (0, 0))],
        out_specs=pl.BlockSpec((tt, d), lambda i, j: (i, 0)),
        scratch_shapes=[pltpu.VMEM((d, tt), F32), pltpu.VMEM((ec, tt), BF16), pltpu.VMEM((ec, tt), F32)],
        compiler_params=_params(("arbitrary", "arbitrary")),
        name="peer_dense",
    )(h2t, u_b, vt_b, cnt, ea, rank, eb, x1, mod, final_g)


def _t5_bucket_np(dist):
    n = np.maximum(dist, 0)
    max_exact = REL_BUCKETS // 2
    nf = np.maximum(n, 1).astype(np.float32)
    large = max_exact + (np.log(nf / np.float32(max_exact)) / np.float32(math.log(REL_MAX_DIST / max_exact))
                         * np.float32(REL_BUCKETS - max_exact)).astype(np.int32)
    return np.where(n < max_exact, n, np.minimum(large, REL_BUCKETS - 1)).astype(np.int32)


def _token_bucket_tiles():
    i = np.arange(TQ)[:, None]
    j = np.arange(TQ)[None, :]
    d0 = i - j
    d1 = TQ + i - j
    d2 = WINDOW + i - j
    assert WINDOW == 2 * TQ and _t5_bucket_np(np.array([TQ + 1]))[0] == REL_BUCKETS - 1
    bkt = _t5_bucket_np(np.stack([d0, d1, d2]))
    valid = np.stack([d0 >= 0, np.ones_like(d0, bool), d2 < WINDOW])
    return np.where(valid, bkt, -1).astype(np.int32)


def _cmp_bucket_tile():
    i = np.arange(TQ)[:, None]
    c = np.arange(CMP_NEAR)[None, :] - CMP_NEAR // 2
    d = i - (c * CMP_STRIDE + CMP_BLOCK - 1)
    assert _t5_bucket_np(np.array([(CMP_NEAR // 2 + 1) * CMP_STRIDE - CMP_BLOCK + 1]))[0] == REL_BUCKETS - 1
    assert (CMP_NEAR // 2) * CMP_STRIDE + CMP_BLOCK - 1 > TQ - 1
    return np.where(d >= 0, _t5_bucket_np(d), REL_BUCKETS - 1).astype(np.int32)


def _block_overlap_t(nb, n_sel):
    cmp_start = np.arange(nb) * CMP_STRIDE
    sel_start = np.arange(n_sel) * SEL_BLOCK
    lo = np.maximum(cmp_start[None, :], sel_start[:, None])
    hi = np.minimum(cmp_start[None, :] + CMP_BLOCK, sel_start[:, None] + SEL_BLOCK)
    ov = np.maximum(hi - lo, 0).astype(np.float32) / CMP_BLOCK
    ov[:, nb - 1] = 0.0
    return ov


def _arrange_w_in(w):
    d = w.shape[0]
    scale = HEAD_DIM ** -0.5
    o_kv = NSA_WIDTH
    o_gate = o_kv + 3 * 2 * NSA_GROUPS * HEAD_DIM
    o_dq = o_gate + NSA_HEADS * 3
    o_dk = o_dq + DIFF_WIDTH
    o_dv = o_dk + DIFF_WIDTH
    z64 = lambda *lead: jnp.zeros((d,) + lead + (HEAD_DIM,), w.dtype)
    qn = (w[:, :o_kv] * scale).reshape(d, NSA_HEADS, HEAD_DIM)
    qn = jnp.concatenate([qn, z64(NSA_HEADS)], axis=-1).reshape(d, NSA_HEADS * LANES)
    kv = w[:, o_kv:o_gate].reshape(d, 3, 2, NSA_GROUPS, HEAD_DIM)
    gates = w[:, o_gate:o_dq].reshape(d, NSA_GROUPS, NSA_HPG * 3)
    dq = (w[:, o_dq:o_dk] * scale).reshape(d, DIFF_HEADS, 2, HEAD_DIM)
    dq = jnp.concatenate([dq[:, :, 0], z64(DIFF_HEADS), z64(DIFF_HEADS), dq[:, :, 1]], axis=-1)
    dq = dq.reshape(d, DIFF_HEADS * 2 * LANES)
    dk = w[:, o_dk:o_dv]
    dv = w[:, o_dv:]
    kvc = kv[:, 0].reshape(d, 2 * NSA_GROUPS * HEAD_DIM)
    kpad = lambda k: jnp.concatenate([k, z64(NSA_GROUPS)], axis=-1).reshape(d, NSA_GROUPS * LANES)
    vdup = lambda v: jnp.concatenate([v, v], axis=-1).reshape(d, NSA_GROUPS * LANES)
    gpad = jnp.pad(gates, ((0, 0), (0, 0), (0, LANES - NSA_HPG * 3))).reshape(d, N_GATE)
    cols = [qn, dq, dk, dv, kvc, kpad(kv[:, 1, 0]), kpad(kv[:, 2, 0]), vdup(kv[:, 1, 1]), vdup(kv[:, 2, 1]), gpad]
    return jnp.concatenate(cols, axis=1).astype(BF16)


def kernel(x, c, rel_bias, ada_w, ada_b, norm1_g, norm2_g, w_in, w_out, cmp_pos, cmp_w1, cmp_w2, lam_q1, lam_k1,
           lam_q2, lam_k2, diff_subln_g, peer_wq, peer_sub_keys, peer_u, peer_v, final_g):
    bsz, seq, d = x.shape
    t = bsz * seq
    nb = seq // CMP_STRIDE
    n_sel = seq // SEL_BLOCK
    assert seq % TK_FAR == 0 and n_sel <= HEAD_DIM
    x2d = x.reshape(t, d)

    mod = _modulation(c, ada_w[0], ada_b[0]).reshape(bsz, 6, d)
    main, gates = _in_proj(x2d, mod, norm1_g[0].reshape(1, d), _arrange_w_in(w_in[0]), seq)

    tokr = (main[:, COL_KVC:COL_KVC + 4 * HEAD_DIM].reshape(bsz, nb, CMP_STRIDE, 4, HEAD_DIM)
            .transpose(0, 3, 1, 2, 4).reshape(bsz, 4, nb, CMP_STRIDE * HEAD_DIM))
    w2dup = jnp.concatenate([cmp_w2[0], cmp_w2[0]], axis=-1).astype(BF16)
    kvcmp = _compress(tokr, cmp_w1[0].astype(BF16), cmp_pos[0].reshape(2, 1, CMP_BLOCK * HEAD_DIM), w2dup)

    bias_t, bias_c = _bias_tables(rel_bias.astype(F32).T.reshape(-1), jnp.asarray(_token_bucket_tiles()),
                                  jnp.asarray(_cmp_bucket_tile()))
    cts_t = jnp.asarray(_block_overlap_t(nb, n_sel))
    place = jnp.asarray(np.eye(n_sel, LANES, HEAD_DIM, dtype=np.float32), BF16)

    o_nsa = _nsa(main, kvcmp, gates, bias_c, bias_t, cts_t, place, bsz, seq)

    lam_rows = jnp.pad(jnp.stack([lam_q1[0], lam_k1[0], lam_q2[0], lam_k2[0]]).astype(F32),
                       ((0, 4), (0, LANES - lam_q1.shape[1])))
    o_d = _diff(main, bias_t, lam_rows, diff_subln_g[0].reshape(1, DIFF_V_DIM), bsz, seq)

    x1, h2t, subt = _post(o_nsa, o_d, x2d, mod, norm2_g[0].reshape(1, d), w_out[0].astype(BF16),
                          peer_wq[0].astype(BF16), peer_sub_keys[0], seq)
    cnt, ea, rank, eb = _peer_select(subt)
    out = _peer_dense(h2t, peer_u[0].astype(BF16), peer_v[0].T.astype(BF16), cnt, ea, rank, eb, x1, mod,
                      final_g.reshape(1, d), seq)
    return out.reshape(bsz, seq, d)
```
